```python
import math
import jax
import jax.numpy as jnp
from jax import lax
import numpy as np

D_MODEL = 2048
BATCH = 4
SEQ = 2048
DEPTH = 4
DEC_BATCH = 8
DEC_SEQ = 4
PAST_LEN = 16384
PAGE_SIZE = 128

HEAD_DIM = 128
H_TOT = D_MODEL // HEAD_DIM
H_A = (3 * H_TOT) // 8
H_B = (H_TOT - H_A) // 2
H_C = H_TOT - H_A - H_B
HD_TOT = H_TOT * HEAD_DIM
DQK_A = HEAD_DIM // 2
ROT_A = DQK_A // 4
ROPE_THETA = 500000.0
Q_BLOCK = 128
IN_COLS = 3 * HD_TOT + H_B
N_EXPERTS = 16
N_GROUPS = 4
EXPERTS_PER_GROUP = N_EXPERTS // N_GROUPS
TOP_K = 2
D_FF = D_MODEL // 2
ALPHA = (2 * DEPTH) ** 0.25
BETA = (8 * DEPTH) ** -0.25
LN_EPS = 1e-5
NEG_INF = -1e30

kernel_name = 'hymba_diff_fox_stickbreak_sharedrouter_moe_step'


def layer_norm(x, g, b):
    xf = x.astype(jnp.float32)
    mu = jnp.mean(xf, axis=-1, keepdims=True)
    var = jnp.mean(jnp.square(xf - mu), axis=-1, keepdims=True)
    return ((xf - mu) * lax.rsqrt(var + LN_EPS) * g + b).astype(x.dtype)


def rms_norm(x, g):
    xf = x.astype(jnp.float32)
    return xf * lax.rsqrt(jnp.mean(jnp.square(xf), axis=-1, keepdims=True) + LN_EPS) * g


def rope_partial(x, pos):
    half = ROT_A // 2
    inv_freq = ROPE_THETA ** (-jnp.arange(half, dtype=jnp.float32) * 2.0 / ROT_A)
    ang = pos.astype(jnp.float32)[:, None] * inv_freq[None, :]
    cos = jnp.cos(ang)[None, :, None, None, :]
    sin = jnp.sin(ang)[None, :, None, None, :]
    x1 = x[..., :half].astype(jnp.float32)
    x2 = x[..., half:ROT_A].astype(jnp.float32)
    rot = jnp.concatenate([x1 * cos - x2 * sin, x2 * cos + x1 * sin], axis=-1).astype(x.dtype)
    return jnp.concatenate([rot, x[..., ROT_A:]], axis=-1)


def sweep_query_blocks(fn, q_side, q_offset):
    B, Tq = q_side[0].shape[:2]
    bq = Q_BLOCK if Tq % Q_BLOCK == 0 else Tq
    nb = Tq // bq
    blocks = tuple(jnp.moveaxis(a.reshape(B, nb, bq, *a.shape[2:]), 1, 0) for a in q_side)
    starts = q_offset + bq * jnp.arange(nb, dtype=jnp.int32)
    out = lax.map(lambda xs: fn(xs[0], xs[1] + jnp.arange(bq, dtype=jnp.int32)), (blocks, starts))
    return jnp.moveaxis(out, 0, 1).reshape(B, Tq, *out.shape[3:])


def differential_attention(q, k, v, lam, q_offset):
    kpos = jnp.arange(k.shape[1], dtype=jnp.int32)
    scale = DQK_A ** -0.5

    def block(qs, qpos):
        (qb,) = qs
        s = jnp.einsum('bqhcd,bkhcd->bhcqk', qb, k, preferred_element_type=jnp.float32) * scale
        causal = kpos[None, :] <= qpos[:, None]
        p = jax.nn.softmax(jnp.where(causal, s, NEG_INF), axis=-1)
        w = p[:, :, 0] - lam * p[:, :, 1]
        return jnp.einsum('bhqk,bkhd->bqhd', w, v, preferred_element_type=jnp.float32)

    return sweep_query_blocks(block, (q,), q_offset)


def forgetting_attention(q, k, v, cum_q, cum_k, q_offset):
    kpos = jnp.arange(k.shape[1], dtype=jnp.int32)
    scale = HEAD_DIM ** -0.5
    cum_kT = jnp.swapaxes(cum_k, 1, 2)

    def block(qs, qpos):
        qb, cqb = qs
        s = jnp.einsum('bqhd,bkhd->bhqk', qb, k, preferred_element_type=jnp.float32) * scale
        s = s + jnp.swapaxes(cqb, 1, 2)[..., :, None] - cum_kT[:, :, None, :]
        causal = kpos[None, :] <= qpos[:, None]
        p = jax.nn.softmax(jnp.where(causal, s, NEG_INF), axis=-1)
        return jnp.einsum('bhqk,bkhd->bqhd', p, v, preferred_element_type=jnp.float32)

    return sweep_query_blocks(block, (q, cum_q), q_offset)


def stick_breaking_attention(q, k, v, q_offset):
    kpos = jnp.arange(k.shape[1], dtype=jnp.int32)
    scale = HEAD_DIM ** -0.5

    def block(qs, qpos):
        (qb,) = qs
        z = jnp.einsum('bqhd,bkhd->bhqk', qb, k, preferred_element_type=jnp.float32) * scale
        earlier = kpos[None, :] < qpos[:, None]
        log_stay = jnp.where(earlier, jax.nn.log_sigmoid(-z), 0.0)
        between = lax.cumsum(log_stay, axis=3, reverse=True) - log_stay
        weight = jnp.where(earlier, jnp.exp(jax.nn.log_sigmoid(z) + between), 0.0)
        return jnp.einsum('bhqk,bkhd->bqhd', weight, v, preferred_element_type=jnp.float32)

    return sweep_query_blocks(block, (q,), q_offset)


def token_mixer(h, pos0, past, p):
    B, T, _ = h.shape
    proj = h @ p['w_in']
    q = proj[..., :HD_TOT].reshape(B, T, H_TOT, HEAD_DIM)
    k = proj[..., HD_TOT:2 * HD_TOT].reshape(B, T, H_TOT, HEAD_DIM)
    v = proj[..., 2 * HD_TOT:3 * HD_TOT].reshape(B, T, H_TOT, HEAD_DIM)
    logf = jax.nn.log_sigmoid(proj[..., 3 * HD_TOT:].astype(jnp.float32) + p['b_forget'])
    pos = pos0 + jnp.arange(T, dtype=jnp.int32)

    qa = rope_partial(q[:, :, :H_A].reshape(B, T, H_A, 2, DQK_A), pos)
    ka = rope_partial(k[:, :, :H_A].reshape(B, T, H_A, 2, DQK_A), pos)
    k_rows = jnp.concatenate([ka.reshape(B, T, H_A, HEAD_DIM), k[:, :, H_A:]], axis=2)

    def with_past(new, which, h0, h1):
        if past is None:
            return new
        pool, page_table, layer = past[which], past[3], past[4]
        rows = pool[layer, page_table, :, h0:h1]
        rows = rows.reshape(rows.shape[0], -1, *rows.shape[3:])
        return jnp.concatenate([rows.astype(new.dtype), new], axis=1)

    ka_all = with_past(k_rows[:, :, :H_A], 0, 0, H_A)
    Tk = ka_all.shape[1]
    ka_all = ka_all.reshape(B, Tk, H_A, 2, DQK_A)
    va_all = with_past(v[:, :, :H_A], 1, 0, H_A)
    oa = differential_attention(qa, ka_all, va_all, p['lam'], pos0)
    oa = rms_norm(oa, p['g_subln']) * (1.0 - p['lam_init'])

    hb0, hb1 = H_A, H_A + H_B
    kb_all = with_past(k[:, :, hb0:hb1], 0, hb0, hb1)
    vb_all = with_past(v[:, :, hb0:hb1], 1, hb0, hb1)
    cum = lax.cumsum(with_past(logf, 2, 0, H_B), axis=1)
    ob = forgetting_attention(q[:, :, hb0:hb1], kb_all, vb_all, cum[:, pos0:], cum, pos0)

    kc_all = with_past(k[:, :, hb1:], 0, hb1, H_TOT)
    vc_all = with_past(v[:, :, hb1:], 1, hb1, H_TOT)
    oc = stick_breaking_attention(q[:, :, hb1:], kc_all, vc_all, pos0)

    o = jnp.concatenate([oa, ob, oc], axis=2).astype(h.dtype).reshape(B, T, HD_TOT)
    return o @ p['w_out'], k_rows, v, logf


def shared_router_moe(h, w_router, b_router, w_gate_up, w_down):
    B, T, D = h.shape
    t = h.reshape(B * T, D)
    probs = jax.nn.softmax((t @ w_router).astype(jnp.float32) + b_router, axis=-1)
    grouped = probs.reshape(-1, N_GROUPS, EXPERTS_PER_GROUP)
    group_score = jnp.sum(lax.top_k(grouped, TOP_K)[0], axis=-1)
    group_sel = jnp.argmax(group_score, axis=-1)
    in_group = (jnp.arange(N_EXPERTS) // EXPERTS_PER_GROUP)[None, :] == group_sel[:, None]
    top_p, top_i = lax.top_k(jnp.where(in_group, probs, -1.0), TOP_K)
    gates = top_p / jnp.sum(top_p, axis=-1, keepdims=True)
    dense_gate = jnp.sum(jax.nn.one_hot(top_i, N_EXPERTS, dtype=jnp.float32) * gates[..., None], axis=1)
    gu = jnp.einsum('nd,edf->nef', t, w_gate_up)
    act = jax.nn.silu(gu[..., :D_FF]) * gu[..., D_FF:] * dense_gate[..., None].astype(gu.dtype)
    y = jnp.einsum('nef,efd->nd', act, w_down)
    return y.astype(h.dtype).reshape(B, T, D)


def decoder_layer(x, pos0, past, p, w_router, b_router):
    mix, k_rows, v_rows, logf = token_mixer(x, pos0, past, p)
    x = layer_norm(ALPHA * x + mix, p['ln1_g'], p['ln1_b'])
    x = layer_norm(ALPHA * x + shared_router_moe(x, w_router, b_router, p['w_gate_up'], p['w_down']),
                   p['ln2_g'], p['ln2_b'])
    return x, k_rows, v_rows, logf


def setup_inputs(seed: int = 0) -> dict:
    key = jax.random.key(seed)
    ks = jax.random.split(key, 26)
    f32 = jnp.float32
    n_pages = PAST_LEN // PAGE_SIZE
    n_used = DEC_BATCH * n_pages
    n_pool = n_used + max(1, n_used // 4)

    def nrm(k, shape, s):
        return jax.random.normal(k, shape, f32) * s

    x_prompt = nrm(ks[0], (BATCH, SEQ, D_MODEL), 1.0)
    x_sample = nrm(ks[1], (DEC_BATCH, DEC_SEQ, D_MODEL), 1.0)
    cache_k = nrm(ks[2], (DEPTH, n_pool, PAGE_SIZE, H_TOT, HEAD_DIM), 1.0)
    cache_v = nrm(ks[3], (DEPTH, n_pool, PAGE_SIZE, H_TOT, HEAD_DIM), BETA)
    cache_logf = jax.nn.log_sigmoid(4.0 + nrm(ks[4], (DEPTH, n_pool, PAGE_SIZE, H_B), 1.0))
    page_table = jax.random.permutation(ks[5], n_pool)[:n_used].reshape(DEC_BATCH, n_pages).astype(jnp.int32)

    s_in = D_MODEL ** -0.5
    w_in = jnp.concatenate([
        nrm(ks[6], (DEPTH, D_MODEL, HD_TOT), s_in),
        nrm(ks[7], (DEPTH, D_MODEL, HD_TOT), s_in),
        nrm(ks[8], (DEPTH, D_MODEL, HD_TOT), s_in * BETA),
        nrm(ks[9], (DEPTH, D_MODEL, H_B), s_in * 0.5)], axis=-1)
    b_forget = 2.0 + 4.0 * jax.random.uniform(ks[10], (DEPTH, H_B), f32)
    lambda_q1 = nrm(ks[11], (DEPTH, DQK_A), 0.1)
    lambda_k1 = nrm(ks[12], (DEPTH, DQK_A), 0.1)
    lambda_q2 = nrm(ks[13], (DEPTH, DQK_A), 0.1)
    lambda_k2 = nrm(ks[14], (DEPTH, DQK_A), 0.1)
    g_subln = 1.0 + nrm(ks[15], (DEPTH, HEAD_DIM), 0.02)
    w_out = nrm(ks[16], (DEPTH, HD_TOT, D_MODEL), HD_TOT ** -0.5 * BETA)
    ln1_g = 1.0 + nrm(ks[17], (DEPTH, D_MODEL), 0.02)
    ln1_b = nrm(ks[18], (DEPTH, D_MODEL), 0.02)
    w_router = nrm(ks[19], (D_MODEL, N_EXPERTS), s_in)
    b_router = nrm(ks[20], (N_EXPERTS,), 0.01)
    w_gate_up = nrm(ks[21], (DEPTH, N_EXPERTS, D_MODEL, 2 * D_FF), s_in)
    w_down = nrm(ks[22], (DEPTH, N_EXPERTS, D_FF, D_MODEL), D_FF ** -0.5 * BETA)
    ln2_g = 1.0 + nrm(ks[23], (DEPTH, D_MODEL), 0.02)
    ln2_b = nrm(ks[24], (DEPTH, D_MODEL), 0.02)
    return {'x_prompt': x_prompt, 'x_sample': x_sample, 'cache_k': cache_k, 'cache_v': cache_v,
            'cache_logf': cache_logf, 'page_table': page_table, 'w_in': w_in, 'b_forget': b_forget,
            'lambda_q1': lambda_q1, 'lambda_k1': lambda_k1, 'lambda_q2': lambda_q2, 'lambda_k2': lambda_k2,
            'g_subln': g_subln, 'w_out': w_out, 'ln1_g': ln1_g, 'ln1_b': ln1_b,
            'w_router': w_router, 'b_router': b_router, 'w_gate_up': w_gate_up, 'w_down': w_down,
            'ln2_g': ln2_g, 'ln2_b': ln2_b}


def reference(x_prompt, x_sample, cache_k, cache_v, cache_logf, page_table, w_in, b_forget,
              lambda_q1, lambda_k1, lambda_q2, lambda_k2, g_subln, w_out, ln1_g, ln1_b,
              w_router, b_router, w_gate_up, w_down, ln2_g, ln2_b):
    f32 = jnp.float32
    y_p, y_s = x_prompt, x_sample
    kp, vp, fp, ksm, vsm, fsm = [], [], [], [], [], []
    for l in range(DEPTH):
        lam_init = 0.8 - 0.6 * math.exp(-0.3 * l)
        lam = (jnp.exp(jnp.sum(lambda_q1[l].astype(f32) * lambda_k1[l].astype(f32)))
               - jnp.exp(jnp.sum(lambda_q2[l].astype(f32) * lambda_k2[l].astype(f32))) + lam_init)
        p = {'w_in': w_in[l], 'b_forget': b_forget[l], 'lam': lam, 'lam_init': lam_init,
             'g_subln': g_subln[l], 'w_out': w_out[l], 'ln1_g': ln1_g[l], 'ln1_b': ln1_b[l],
             'w_gate_up': w_gate_up[l], 'w_down': w_down[l], 'ln2_g': ln2_g[l], 'ln2_b': ln2_b[l]}
        y_p, k_r, v_r, f_r = decoder_layer(y_p, 0, None, p, w_router, b_router)
        kp.append(k_r)
        vp.append(v_r)
        fp.append(f_r)
        y_s, k_r, v_r, f_r = decoder_layer(y_s, PAST_LEN, (cache_k, cache_v, cache_logf, page_table, l),
                                           p, w_router, b_router)
        ksm.append(k_r)
        vsm.append(v_r)
        fsm.append(f_r)
    return (y_p, y_s, jnp.stack(kp), jnp.stack(vp), jnp.stack(fp), jnp.stack(ksm), jnp.stack(vsm), jnp.stack(fsm))
```

```python
import functools
import math

import jax
import jax.numpy as jnp
from jax import lax
from jax.experimental import pallas as pl
from jax.experimental.pallas import tpu as pltpu

F32 = jnp.float32
BF16 = jnp.bfloat16
I32 = jnp.int32

HEAD_DIM = 128
H_TOT = 16
H_A = 6
H_B = 5
H_C = 5
D_MODEL = H_TOT * HEAD_DIM
DQK_A = HEAD_DIM // 2
ROT_A = DQK_A // 4
ROPE_THETA = 500000.0
N_EXPERTS = 16
N_GROUPS = 4
EXPERTS_PER_GROUP = 4
D_FF = D_MODEL // 2
LN_EPS = 1e-5
NEG = -1e30

LANES = 128
ROW_TILE = 256
HEADS_PER_STEP = 4
ATT_BLOCK = 512
SB_KBLOCK = 256
PAGES_PER_STEP = 4
COLS_PER_HEAD = 8
VMEM_LIMIT = 56 * 1024 * 1024


def _dot(a, b):
    return jnp.dot(a, b, preferred_element_type=F32)


def _dot_nt(a, b):
    return lax.dot_general(a, b, (((1,), (1,)), ((), ())), preferred_element_type=F32)


def _log_sigmoid(x):
    return jnp.minimum(x, 0.0) - jnp.log1p(jnp.exp(-jnp.abs(x)))


def _split2(x):
    hi = x.astype(BF16)
    lo = (x - hi.astype(F32)).astype(BF16)
    return hi, lo


def _layer_norm(y, g, b):
    mu = jnp.mean(y, axis=-1, keepdims=True)
    d = y - mu
    var = jnp.mean(d * d, axis=-1, keepdims=True)
    return d * lax.rsqrt(var + LN_EPS) * g + b


def _prep_kernel(x_ref, wf_ref, bf_ref, xb_ref, lf_ref, lft_ref):
    xb = x_ref[...].astype(BF16)
    xb_ref[...] = xb
    lf = _log_sigmoid(_dot(xb, wf_ref[...]) + bf_ref[...])
    lf_ref[...] = lf
    lft_ref[...] = lf.T[:8, :]


def _prep(x, wf, bfg):
    n = x.shape[0]
    nt = n // ROW_TILE
    return pl.pallas_call(
        _prep_kernel,
        grid=(nt,),
        in_specs=[pl.BlockSpec((ROW_TILE, D_MODEL), lambda i: (i, 0)),
                  pl.BlockSpec((D_MODEL, LANES), lambda i: (0, 0)),
                  pl.BlockSpec((1, LANES), lambda i: (0, 0))],
        out_specs=[pl.BlockSpec((ROW_TILE, D_MODEL), lambda i: (i, 0)),
                   pl.BlockSpec((ROW_TILE, LANES), lambda i: (i, 0)),
                   pl.BlockSpec((8, ROW_TILE), lambda i: (0, i))],
        out_shape=[jax.ShapeDtypeStruct((n, D_MODEL), BF16),
                   jax.ShapeDtypeStruct((n, LANES), F32),
                   jax.ShapeDtypeStruct((8, n), F32)],
        compiler_params=pltpu.CompilerParams(dimension_semantics=("parallel",)),
        name="prep",
    )(x, wf, bfg)


def _qkv_kernel(xb_ref, wq_ref, wk_ref, wv_ref, c_ref, sa_ref, sb_ref,
                q_ref, k_ref, v_ref, kb_ref, vb_ref):
    j = pl.program_id(0)
    xb = xb_ref[...]
    q = _dot(xb, wq_ref[...])
    k = _dot(xb, wk_ref[...])
    v = _dot(xb, wv_ref[...])
    v_ref[...] = v
    vb_ref[...] = v.astype(BF16)
    scale_a = DQK_A ** -0.5
    scale_bc = HEAD_DIM ** -0.5

    def rope(t):
        return (t * c_ref[...] + pltpu.roll(t, LANES - ROT_A // 2, 1) * sa_ref[...]
                + pltpu.roll(t, ROT_A // 2, 1) * sb_ref[...])

    @pl.when(j * HEADS_PER_STEP < H_A)
    def _():
        for hh in range(HEADS_PER_STEP):
            sl = slice(hh * HEAD_DIM, (hh + 1) * HEAD_DIM)
            use = (j * HEADS_PER_STEP + hh) < H_A
            qh = q[:, sl]
            kh = k[:, sl]
            qh = jnp.where(use, rope(qh), qh)
            kh = jnp.where(use, rope(kh), kh)
            scale = jnp.where(use, scale_a, scale_bc)
            q_ref[:, sl] = (qh * scale).astype(BF16)
            k_ref[:, sl] = kh
            kb_ref[:, sl] = kh.astype(BF16)

    @pl.when(j * HEADS_PER_STEP >= H_A)
    def _():
        q_ref[...] = (q * scale_bc).astype(BF16)
        k_ref[...] = k
        kb_ref[...] = k.astype(BF16)


def _qkv(xb, wq, wk, wv, rope_c, rope_sa, rope_sb, n_prompt_tiles, tiles_per_seq):
    n = xb.shape[0]
    nt = n // ROW_TILE
    wcol = HEADS_PER_STEP * HEAD_DIM
    nj = D_MODEL // wcol

    def tbl(j, i):
        return (jnp.where(i < n_prompt_tiles, i % tiles_per_seq, tiles_per_seq), 0)

    wspec = pl.BlockSpec((D_MODEL, wcol), lambda j, i: (0, j))
    ospec = pl.BlockSpec((ROW_TILE, wcol), lambda j, i: (i, j))
    tspec = pl.BlockSpec((ROW_TILE, LANES), tbl)
    return pl.pallas_call(
        _qkv_kernel,
        grid=(nj, nt),
        in_specs=[pl.BlockSpec((ROW_TILE, D_MODEL), lambda j, i: (i, 0)),
                  wspec, wspec, wspec, tspec, tspec, tspec],
        out_specs=[ospec, ospec, ospec, ospec, ospec],
        out_shape=[jax.ShapeDtypeStruct((n, D_MODEL), BF16),
                   jax.ShapeDtypeStruct((n, D_MODEL), F32),
                   jax.ShapeDtypeStruct((n, D_MODEL), F32),
                   jax.ShapeDtypeStruct((n, D_MODEL), BF16),
                   jax.ShapeDtypeStruct((n, D_MODEL), BF16)],
        compiler_params=pltpu.CompilerParams(dimension_semantics=("parallel", "parallel"),
                                             vmem_limit_bytes=VMEM_LIMIT),
        name="qkv",
    )(xb, wq, wk, wv, rope_c, rope_sa, rope_sb)


def _cumsum_kernel(lft_ref, u_ref, out_ref):
    t = lft_ref.shape[1]
    blk = u_ref.shape[0]
    u = u_ref[...]
    carry = jnp.zeros((8, 1), F32)
    for b in range(t // blk):
        x = lft_ref[:, b * blk:(b + 1) * blk]
        hi = x.astype(BF16)
        r = x - hi.astype(F32)
        mid = r.astype(BF16)
        lo = (r - mid.astype(F32)).astype(BF16)
        cs = _dot(hi, u) + _dot(mid, u) + _dot(lo, u) + carry
        out_ref[:, b * blk:(b + 1) * blk] = cs
        carry = cs[:, blk - 1:blk]


def _cumsum(lft, u_incl, batch, seq):
    return pl.pallas_call(
        _cumsum_kernel,
        grid=(batch,),
        in_specs=[pl.BlockSpec((8, seq), lambda b: (0, b)),
                  pl.BlockSpec(u_incl.shape, lambda b: (0, 0))],
        out_specs=pl.BlockSpec((8, seq), lambda b: (0, b)),
        out_shape=jax.ShapeDtypeStruct((8, batch * seq), F32),
        compiler_params=pltpu.CompilerParams(dimension_semantics=("parallel",)),
        name="cumsum",
    )(lft, u_incl)


def _softmax_update(s, m, l, acc, v):
    m_new = jnp.maximum(m, jnp.max(s, axis=1, keepdims=True))
    alpha = jnp.exp(m - m_new)
    p = jnp.exp(s - m_new)
    l = alpha * l + jnp.sum(p, axis=1, keepdims=True)
    acc = alpha * acc + _dot(p.astype(BF16), v)
    return m_new, l, acc


def _attn_kernel(scal_ref, q_ref, k_ref, v_ref, cum_ref, u_ref,
                 lq1_ref, lk1_ref, lq2_ref, lk2_ref, g_ref, o_ref, *, tq, tkc):
    h = pl.program_id(1)
    qi = pl.program_id(2)
    lam_init = scal_ref[0]

    def kv(kb, width):
        start = pl.multiple_of(kb * width, width)
        return k_ref[pl.ds(start, width), :], v_ref[pl.ds(start, width), :]

    def causal(width_k, kb, strict):
        row = qi * tq + lax.broadcasted_iota(I32, (tq, width_k), 0)
        col = kb * width_k + lax.broadcasted_iota(I32, (tq, width_k), 1)
        return (col < row) if strict else (col <= row)

    def init():
        return (jnp.full((tq, 1), NEG, F32), jnp.zeros((tq, 1), F32),
                jnp.zeros((tq, HEAD_DIM), F32))

    @pl.when(h < H_A)
    def _():
        q = q_ref[...]
        lane = lax.broadcasted_iota(I32, q.shape, 1)
        zero = jnp.zeros_like(q)
        q0 = jnp.where(lane < DQK_A, q, zero)
        q1 = jnp.where(lane >= DQK_A, q, zero)

        def step(kb, carry, masked):
            c0, c1 = carry
            k, v = kv(kb, tq)
            s0 = _dot_nt(q0, k)
            s1 = _dot_nt(q1, k)
            if masked:
                ok = causal(tq, kb, False)
                s0 = jnp.where(ok, s0, NEG)
                s1 = jnp.where(ok, s1, NEG)
            return _softmax_update(s0, *c0, v), _softmax_update(s1, *c1, v)

        carry = lax.fori_loop(0, qi, lambda kb, c: step(kb, c, False), (init(), init()))
        (_, l0, a0), (_, l1, a1) = step(qi, carry, True)
        lam = (jnp.exp(jnp.sum(lq1_ref[...] * lk1_ref[...], axis=1, keepdims=True))
               - jnp.exp(jnp.sum(lq2_ref[...] * lk2_ref[...], axis=1, keepdims=True)) + lam_init)
        o = a0 / l0 - lam * (a1 / l1)
        o = o * lax.rsqrt(jnp.mean(o * o, axis=1, keepdims=True) + LN_EPS) * g_ref[...]
        o_ref[...] = (o * (1.0 - lam_init)).astype(o_ref.dtype)

    @pl.when((h >= H_A) & (h < H_A + H_B))
    def _():
        q = q_ref[...]
        hb = h - H_A

        def step(kb, carry, masked):
            k, v = kv(kb, tq)
            start = pl.multiple_of(kb * tq, tq)
            s = _dot_nt(q, k) - cum_ref[pl.ds(hb, 1), pl.ds(start, tq)]
            if masked:
                s = jnp.where(causal(tq, kb, False), s, NEG)
            return _softmax_update(s, *carry, v)

        carry = lax.fori_loop(0, qi, lambda kb, c: step(kb, c, False), init())
        _, l, acc = step(qi, carry, True)
        o_ref[...] = (acc / l).astype(o_ref.dtype)

    @pl.when(h >= H_A + H_B)
    def _():
        q = q_ref[...]
        u = u_ref[...]
        ratio = tq // tkc

        def step(kb, carry, masked):
            run, acc = carry
            k, v = kv(kb, tkc)
            z = _dot_nt(q, k)
            lsz = _log_sigmoid(z)
            ls = lsz - z
            if masked:
                ok = causal(tkc, kb, True)
                ls = jnp.where(ok, ls, 0.0)
            hi, lo = _split2(ls)
            between = _dot(hi, u) + _dot(lo, u) + run
            w = jnp.exp(lsz + between)
            if masked:
                w = jnp.where(ok, w, 0.0)
            acc = acc + _dot(w.astype(BF16), v)
            run = run + jnp.sum(ls, axis=1, keepdims=True)
            return run, acc

        carry = (jnp.zeros((tq, 1), F32), jnp.zeros((tq, HEAD_DIM), F32))
        for d in range(ratio):
            carry = step(qi * ratio + (ratio - 1 - d), carry, True)
        nfull = qi * ratio
        carry = lax.fori_loop(0, nfull, lambda i, c: step(nfull - 1 - i, c, False), carry)
        o_ref[...] = carry[1].astype(o_ref.dtype)


def _attn(scal, qb, kb, vb, cumt, u_strict, lq1, lk1, lq2, lk2, g, batch, seq):
    tq = min(ATT_BLOCK, seq)
    tkc = min(SB_KBLOCK, tq)
    nq = seq // tq
    lspec = pl.BlockSpec((1, DQK_A), lambda b, h, i: (0, 0))
    return pl.pallas_call(
        functools.partial(_attn_kernel, tq=tq, tkc=tkc),
        grid=(batch, H_TOT, nq),
        in_specs=[pl.BlockSpec(memory_space=pltpu.SMEM),
                  pl.BlockSpec((tq, HEAD_DIM), lambda b, h, i: (b * nq + i, h)),
                  pl.BlockSpec((seq, HEAD_DIM), lambda b, h, i: (b, h)),
                  pl.BlockSpec((seq, HEAD_DIM), lambda b, h, i: (b, h)),
                  pl.BlockSpec((8, seq), lambda b, h, i: (0, b)),
                  pl.BlockSpec((tkc, tkc), lambda b, h, i: (0, 0)),
                  lspec, lspec, lspec, lspec,
                  pl.BlockSpec((1, HEAD_DIM), lambda b, h, i: (0, 0))],
        out_specs=pl.BlockSpec((tq, HEAD_DIM), lambda b, h, i: (b * nq + i, h)),
        out_shape=jax.ShapeDtypeStruct((batch * seq, D_MODEL), BF16),
        compiler_params=pltpu.CompilerParams(
            dimension_semantics=("parallel", "parallel", "parallel"),
            vmem_limit_bytes=VMEM_LIMIT),
        name="attn",
    )(scal, qb, kb, vb, cumt, u_strict, lq1, lk1, lq2, lk2, g)


def _sattn_kernel(*refs, n_pages_step, page):
    g_n = n_pages_step
    pt_ref, scal_ref = refs[0], refs[1]
    k_refs = refs[2:2 + g_n]
    v_refs = refs[2 + g_n:2 + 2 * g_n]
    f_refs = refs[2 + 2 * g_n:2 + 3 * g_n]
    (qbd_ref, kn_ref, vn_ref, lfn_ref, et_ref, ut_ref,
     lq1_ref, lk1_ref, lq2_ref, lk2_ref, gs_ref,
     o_ref, m_scr, l_scr, run_scr, acc_scr, vb_scr, st_scr, lft_scr, fpad_scr) = refs[2 + 3 * g_n:]
    del pt_ref
    t = pl.program_id(1)
    nsteps = pl.num_programs(1)
    lam_init = scal_ref[0]
    ncol = H_TOT * COLS_PER_HEAD
    col_a = H_A * COLS_PER_HEAD
    col_b = (H_A + H_B) * COLS_PER_HEAD

    def process(width, valid_ab, valid_c):
        st = st_scr[:, :width]
        lft = lft_scr[:, :width]
        row = lax.broadcasted_iota(I32, (ncol, width), 0)
        is_a = row < col_a
        is_c = row >= col_b
        lsz = _log_sigmoid(st)
        ls = lsz - st
        if valid_c is not None:
            ls = jnp.where(valid_c, ls, 0.0)
        x = jnp.where(is_c, ls, lft)
        hi, lo = _split2(x)
        ut = ut_ref[:width, :width]
        suffix = _dot(hi, ut) + _dot(lo, ut) + run_scr[...]
        logit = jnp.where(is_a, st, jnp.where(is_c, lsz, st) + suffix)
        if valid_ab is not None:
            logit = jnp.where((is_c & valid_c) | (~is_c & valid_ab), logit, NEG)
        m_old = m_scr[...]
        m_new = jnp.where(is_c[:, :1], 0.0,
                          jnp.maximum(m_old, jnp.max(logit, axis=1, keepdims=True)))
        p = jnp.exp(logit - m_new)
        alpha = jnp.exp(m_old - m_new)
        l_scr[...] = alpha * l_scr[...] + jnp.sum(p, axis=1, keepdims=True)
        m_scr[...] = m_new
        run_scr[...] = run_scr[...] + jnp.sum(x, axis=1, keepdims=True)
        pb = p.astype(BF16)
        for hp in range(H_TOT // 2):
            rs = slice(2 * COLS_PER_HEAD * hp, 2 * COLS_PER_HEAD * (hp + 1))
            cs = slice(2 * HEAD_DIM * hp, 2 * HEAD_DIM * (hp + 1))
            acc_scr[rs, :] = alpha[rs, :] * acc_scr[rs, :] + _dot(pb[rs, :], vb_scr[:width, cs])

    @pl.when(t == 0)
    def _():
        row = lax.broadcasted_iota(I32, (ncol, 1), 0)
        m_scr[...] = jnp.where(row >= col_b, 0.0, NEG)
        l_scr[...] = jnp.zeros_like(l_scr)
        run_scr[...] = jnp.zeros_like(run_scr)
        acc_scr[...] = jnp.zeros_like(acc_scr)
        fpad_scr[...] = jnp.zeros_like(fpad_scr)
        st_scr[:, :page] = _dot(kn_ref[0], qbd_ref[0]).T
        lft_scr[:, :page] = lfn_ref[0]
        vb_scr[:page, :] = vn_ref[0]
        qidx = lax.broadcasted_iota(I32, (ncol, page), 0) % 4
        key = lax.broadcasted_iota(I32, (ncol, page), 1)
        process(page, key <= qidx, key < qidx)

    for g in range(g_n):
        sl = slice(g * page, (g + 1) * page)
        st_scr[:, sl] = _dot(k_refs[g][0, 0].astype(BF16), qbd_ref[0]).T
        fpad_scr[:, :H_B] = f_refs[g][0, 0]
        fh, fl = _split2(fpad_scr[...])
        lft_scr[:, sl] = _dot_nt(et_ref[...], fh) + _dot_nt(et_ref[...], fl)
        vb_scr[sl, :] = v_refs[g][0, 0].astype(BF16)
    process(g_n * page, None, None)

    @pl.when(t == nsteps - 1)
    def _():
        row = lax.broadcasted_iota(I32, (ncol, 1), 0)
        linv = jnp.where(row >= col_b, 1.0, 1.0 / l_scr[...])
        lam = (jnp.exp(jnp.sum(lq1_ref[...] * lk1_ref[...], axis=1, keepdims=True))
               - jnp.exp(jnp.sum(lq2_ref[...] * lk2_ref[...], axis=1, keepdims=True)) + lam_init)
        half = COLS_PER_HEAD // 2
        for h in range(H_TOT):
            rs = slice(COLS_PER_HEAD * h, COLS_PER_HEAD * (h + 1))
            cs = slice(HEAD_DIM * (h % 2), HEAD_DIM * (h % 2 + 1))
            blk = acc_scr[rs, cs] * linv[rs, :]
            if h < H_A:
                o = blk - lam * pltpu.roll(blk, half, 0)
                o = o * lax.rsqrt(jnp.mean(o * o, axis=1, keepdims=True) + LN_EPS) * gs_ref[...]
                blk = o * (1.0 - lam_init)
            o_ref[0, h] = blk


def _sattn(page_table, scal, layer, ck, cv, cf, qbd, kn, vn, lfn, et, ut_strict,
           lq1, lk1, lq2, lk2, g):
    nb, n_pages = page_table.shape
    page = ck.shape[2]
    g_n = PAGES_PER_STEP
    nsteps = n_pages // g_n
    ncol = H_TOT * COLS_PER_HEAD

    def page_map(gg):
        def f(b, t, pt):
            return (layer, pt[b, (nsteps - 1 - t) * g_n + gg], 0, 0)
        return f

    kspecs = [pl.BlockSpec((1, 1, page, D_MODEL), page_map(gg)) for gg in range(g_n)]
    fspecs = [pl.BlockSpec((1, 1, page, H_B), page_map(gg)) for gg in range(g_n)]
    per_b3 = lambda b, t, pt: (b, 0, 0)
    const2 = lambda b, t, pt: (0, 0)
    lspec = pl.BlockSpec((1, DQK_A), const2)
    in_specs = ([pl.BlockSpec(memory_space=pltpu.SMEM)] + kspecs + kspecs + fspecs + [
        pl.BlockSpec((1, D_MODEL, ncol), per_b3),
        pl.BlockSpec((1, page, D_MODEL), per_b3),
        pl.BlockSpec((1, page, D_MODEL), per_b3),
        pl.BlockSpec((1, ncol, page), per_b3),
        pl.BlockSpec((ncol, LANES), const2),
        pl.BlockSpec(ut_strict.shape, const2),
        lspec, lspec, lspec, lspec,
        pl.BlockSpec((1, HEAD_DIM), const2)])
    width = g_n * page
    grid_spec = pltpu.PrefetchScalarGridSpec(
        num_scalar_prefetch=1,
        grid=(nb, nsteps),
        in_specs=in_specs,
        out_specs=pl.BlockSpec((1, H_TOT, COLS_PER_HEAD, HEAD_DIM), lambda b, t, pt: (b, 0, 0, 0)),
        scratch_shapes=[pltpu.VMEM((ncol, 1), F32), pltpu.VMEM((ncol, 1), F32),
                        pltpu.VMEM((ncol, 1), F32), pltpu.VMEM((ncol, 2 * HEAD_DIM), F32),
                        pltpu.VMEM((width, D_MODEL), BF16),
                        pltpu.VMEM((ncol, width), F32), pltpu.VMEM((ncol, width), F32),
                        pltpu.VMEM((page, LANES), F32)])
    return pl.pallas_call(
        functools.partial(_sattn_kernel, n_pages_step=g_n, page=page),
        grid_spec=grid_spec,
        out_shape=jax.ShapeDtypeStruct((nb, H_TOT, COLS_PER_HEAD, HEAD_DIM), F32),
        compiler_params=pltpu.CompilerParams(dimension_semantics=("parallel", "arbitrary"),
                                             vmem_limit_bytes=VMEM_LIMIT),
        name="sattn",
    )(page_table, scal, *([ck] * g_n), *([cv] * g_n), *([cf] * g_n),
      qbd, kn, vn, lfn, et, ut_strict, lq1, lk1, lq2, lk2, g)


def _oproj_kernel(o_ref, wo_ref, x_ref, g_ref, b_ref, wrh_ref, wrl_ref, br_ref,
                  x1_ref, lg_ref, *, alpha):
    mix = _dot(o_ref[...], wo_ref[...])
    x1 = _layer_norm(alpha * x_ref[...] + mix, g_ref[...], b_ref[...])
    x1_ref[...] = x1
    xh, xl = _split2(x1)
    wrh = wrh_ref[...]
    lg_ref[...] = _dot(xh, wrh) + _dot(xl, wrh) + _dot(xh, wrl_ref[...]) + br_ref[...]


def _oproj(o, wo, x, g, b, wrh, wrl, br, alpha):
    n = x.shape[0]
    nt = n // ROW_TILE
    row = lambda i: (i, 0)
    const = lambda i: (0, 0)
    return pl.pallas_call(
        functools.partial(_oproj_kernel, alpha=alpha),
        grid=(nt,),
        in_specs=[pl.BlockSpec((ROW_TILE, D_MODEL), row),
                  pl.BlockSpec((D_MODEL, D_MODEL), const),
                  pl.BlockSpec((ROW_TILE, D_MODEL), row),
                  pl.BlockSpec((1, D_MODEL), const),
                  pl.BlockSpec((1, D_MODEL), const),
                  pl.BlockSpec((D_MODEL, LANES), const),
                  pl.BlockSpec((D_MODEL, LANES), const),
                  pl.BlockSpec((1, LANES), const)],
        out_specs=[pl.BlockSpec((ROW_TILE, D_MODEL), row),
                   pl.BlockSpec((ROW_TILE, LANES), row)],
        out_shape=[jax.ShapeDtypeStruct((n, D_MODEL), F32),
                   jax.ShapeDtypeStruct((n, LANES), F32)],
        compiler_params=pltpu.CompilerParams(dimension_semantics=("parallel",),
                                             vmem_limit_bytes=VMEM_LIMIT),
        name="oproj",
    )(o, wo, x, g, b, wrh, wrl, br)


def _router_kernel(lg_ref, low_ref, route_ref, cnt_ref, run_scr, *, n_real):
    i = pl.program_id(0)
    tm = lg_ref.shape[0]

    @pl.when(i == 0)
    def _():
        run_scr[...] = jnp.zeros_like(run_scr)

    lg = lg_ref[...]
    lane = lax.broadcasted_iota(I32, lg.shape, 1)
    lane_f = lane.astype(F32)
    tok = i * tm + lax.broadcasted_iota(I32, lg.shape, 0)
    is_e = lane < N_EXPERTS
    lgm = jnp.where(is_e, lg, NEG)
    ex = jnp.where(is_e, jnp.exp(lgm - jnp.max(lgm, axis=1, keepdims=True)), 0.0)
    probs = ex / jnp.sum(ex, axis=1, keepdims=True)

    best = None
    for grp in range(N_GROUPS):
        ing = (lane >= grp * EXPERTS_PER_GROUP) & (lane < (grp + 1) * EXPERTS_PER_GROUP)
        pg = jnp.where(ing, probs, -1.0)
        p1 = jnp.max(pg, axis=1, keepdims=True)
        i1 = jnp.min(jnp.where(pg == p1, lane_f, 1e9), axis=1, keepdims=True)
        pg2 = jnp.where(lane_f == i1, -1.0, pg)
        p2 = jnp.max(pg2, axis=1, keepdims=True)
        i2 = jnp.min(jnp.where(pg2 == p2, lane_f, 1e9), axis=1, keepdims=True)
        cand = (p1 + p2, p1, i1, p2, i2)
        if best is None:
            best = cand
        else:
            better = cand[0] > best[0]
            best = tuple(jnp.where(better, c, o) for c, o in zip(cand, best))
    _, p1, i1, p2, i2 = best
    den = p1 + p2
    g1 = p1 / den
    g2 = p2 / den

    sel1 = lane_f == i1
    sel2 = lane_f == i2
    oh = jnp.where((sel1 | sel2) & (tok < n_real), 1.0, 0.0)
    total = _dot(low_ref[...], oh.astype(BF16)) + run_scr[...]
    r1 = jnp.sum(jnp.where(sel1, total, 0.0), axis=1, keepdims=True)
    r2 = jnp.sum(jnp.where(sel2, total, 0.0), axis=1, keepdims=True)
    run = run_scr[...] + jnp.sum(oh, axis=0, keepdims=True)
    run_scr[...] = run
    cnt_ref[...] = jnp.broadcast_to(run, cnt_ref.shape)
    route = jnp.zeros_like(lg)
    for pos, val in enumerate((i1, i2, g1, g2, r1, r2)):
        route = jnp.where(lane == pos, val, route)
    route_ref[...] = route


def _router(logits, low_strict, n_real):
    n = logits.shape[0]
    nt = n // ROW_TILE
    return pl.pallas_call(
        functools.partial(_router_kernel, n_real=n_real),
        grid=(nt,),
        in_specs=[pl.BlockSpec((ROW_TILE, LANES), lambda i: (i, 0)),
                  pl.BlockSpec((ROW_TILE, ROW_TILE), lambda i: (0, 0))],
        out_specs=[pl.BlockSpec((ROW_TILE, LANES), lambda i: (i, 0)),
                   pl.BlockSpec((8, LANES), lambda i: (0, 0))],
        out_shape=[jax.ShapeDtypeStruct((n, LANES), F32),
                   jax.ShapeDtypeStruct((8, LANES), F32)],
        scratch_shapes=[pltpu.VMEM((1, LANES), F32)],
        compiler_params=pltpu.CompilerParams(dimension_semantics=("arbitrary",)),
        name="router",
    )(logits, low_strict)


def _row_gather_start(src_hbm, dst, sem, idx_ref, base, n_rows):
    def body(r, c):
        pltpu.make_async_copy(src_hbm.at[pl.ds(idx_ref[base + r], 1), :],
                              dst.at[pl.ds(r, 1), :], sem).start()
        return c
    lax.fori_loop(0, n_rows, body, 0)


def _row_gather_wait(src_hbm, dst, sem, n_rows):
    def body(r, c):
        pltpu.make_async_copy(src_hbm.at[pl.ds(0, 1), :], dst.at[pl.ds(r, 1), :], sem).wait()
        return c
    lax.fori_loop(0, n_rows, body, 0)


def _experts_kernel(te_ref, nu_ref, tok_ref, x_hbm, wgu_ref, wd_ref, ys_ref, xbuf, sem):
    del te_ref
    j = pl.program_id(0)
    nt = pl.num_programs(0)
    tm = ys_ref.shape[0]
    n_used = nu_ref[0]
    slot = j % 2

    @pl.when(j == 0)
    def _():
        _row_gather_start(x_hbm, xbuf.at[0], sem.at[0], tok_ref, 0, tm)

    @pl.when((j + 1 < nt) & (j + 1 < n_used))
    def _():
        _row_gather_start(x_hbm, xbuf.at[1 - slot], sem.at[1 - slot], tok_ref, (j + 1) * tm, tm)

    @pl.when(j < n_used)
    def _():
        _row_gather_wait(x_hbm, xbuf.at[slot], sem.at[slot], tm)
        xb = xbuf[slot].astype(BF16)
        hgu = _dot(xb, wgu_ref[0])
        gate = hgu[:, :D_FF]
        act = gate * jax.nn.sigmoid(gate) * hgu[:, D_FF:]
        ys_ref[...] = _dot(act.astype(BF16), wd_ref[0])

    @pl.when(j >= n_used)
    def _():
        ys_ref[...] = jnp.zeros_like(ys_ref)


def _experts(tile_expert, n_used, tok_of_slot, x1, wgu, wd, n_tiles):
    grid_spec = pltpu.PrefetchScalarGridSpec(
        num_scalar_prefetch=3,
        grid=(n_tiles,),
        in_specs=[pl.BlockSpec(memory_space=pl.ANY),
                  pl.BlockSpec((1, D_MODEL, 2 * D_FF), lambda j, te, nu, tk: (te[j], 0, 0)),
                  pl.BlockSpec((1, D_FF, D_MODEL), lambda j, te, nu, tk: (te[j], 0, 0))],
        out_specs=pl.BlockSpec((ROW_TILE, D_MODEL), lambda j, te, nu, tk: (j, 0)),
        scratch_shapes=[pltpu.VMEM((2, ROW_TILE, D_MODEL), F32),
                        pltpu.SemaphoreType.DMA((2,))])
    return pl.pallas_call(
        _experts_kernel,
        grid_spec=grid_spec,
        out_shape=jax.ShapeDtypeStruct((n_tiles * ROW_TILE, D_MODEL), F32),
        compiler_params=pltpu.CompilerParams(dimension_semantics=("arbitrary",),
                                             vmem_limit_bytes=VMEM_LIMIT),
        name="experts",
    )(tile_expert, n_used, tok_of_slot, x1, wgu, wd)


def _combine_kernel(s1_ref, s2_ref, x1_ref, route_ref, g_ref, b_ref, ys_hbm, out_ref,
                    ybuf, sem, *, alpha):
    i = pl.program_id(0)
    tm = out_ref.shape[0]
    _row_gather_start(ys_hbm, ybuf.at[0], sem.at[0], s1_ref, i * tm, tm)
    _row_gather_start(ys_hbm, ybuf.at[1], sem.at[1], s2_ref, i * tm, tm)
    route = route_ref[...]
    g1 = route[:, 2:3]
    g2 = route[:, 3:4]
    _row_gather_wait(ys_hbm, ybuf.at[0], sem.at[0], tm)
    _row_gather_wait(ys_hbm, ybuf.at[1], sem.at[1], tm)
    y = g1 * ybuf[0] + g2 * ybuf[1]
    out_ref[...] = _layer_norm(alpha * x1_ref[...] + y, g_ref[...], b_ref[...])


def _combine(slot1, slot2, x1, route, g, b, ys, alpha):
    n = x1.shape[0]
    nt = n // ROW_TILE
    row = lambda i, s1, s2: (i, 0)
    const = lambda i, s1, s2: (0, 0)
    grid_spec = pltpu.PrefetchScalarGridSpec(
        num_scalar_prefetch=2,
        grid=(nt,),
        in_specs=[pl.BlockSpec((ROW_TILE, D_MODEL), row),
                  pl.BlockSpec((ROW_TILE, LANES), row),
                  pl.BlockSpec((1, D_MODEL), const),
                  pl.BlockSpec((1, D_MODEL), const),
                  pl.BlockSpec(memory_space=pl.ANY)],
        out_specs=pl.BlockSpec((ROW_TILE, D_MODEL), row),
        scratch_shapes=[pltpu.VMEM((2, ROW_TILE, D_MODEL), F32),
                        pltpu.SemaphoreType.DMA((2,))])
    return pl.pallas_call(
        functools.partial(_combine_kernel, alpha=alpha),
        grid_spec=grid_spec,
        out_shape=jax.ShapeDtypeStruct((n, D_MODEL), F32),
        compiler_params=pltpu.CompilerParams(dimension_semantics=("arbitrary",),
                                             vmem_limit_bytes=VMEM_LIMIT),
        name="combine",
    )(slot1, slot2, x1, route, g, b, ys)


def _rope_tables(pos):
    half = ROT_A // 2
    inv_freq = ROPE_THETA ** (-jnp.arange(half, dtype=F32) * 2.0 / ROT_A)
    ang = pos.astype(F32)[:, None] * inv_freq[None, :]
    cos, sin = jnp.cos(ang), jnp.sin(ang)
    lane = jnp.arange(LANES)
    m = lane % half
    in_x1 = (lane % DQK_A) < half
    in_x2 = ((lane % DQK_A) >= half) & ((lane % DQK_A) < ROT_A)
    c = jnp.where((in_x1 | in_x2)[None, :], cos[:, m], 1.0)
    sa = jnp.where(in_x1[None, :], -sin[:, m], 0.0)
    sb = jnp.where(in_x2[None, :], sin[:, m], 0.0)
    return c, sa, sb


def _tri(n, fn):
    r = jnp.arange(n)[:, None]
    c = jnp.arange(n)[None, :]
    return fn(r, c).astype(BF16)


def _sample_col_layout():
    ncol = H_TOT * COLS_PER_HEAD
    col = jnp.arange(ncol)
    head = col // COLS_PER_HEAD
    r = col % COLS_PER_HEAD
    is_a = head < H_A
    used = is_a | (r < COLS_PER_HEAD // 2)
    lo = head * HEAD_DIM + jnp.where(is_a, (r // 4) * DQK_A, 0)
    hi = lo + jnp.where(is_a, DQK_A, HEAD_DIM)
    feat = jnp.arange(D_MODEL)[:, None]
    mask = (feat >= lo[None, :]) & (feat < hi[None, :]) & used[None, :]
    return mask, r % 4


def kernel(x_prompt, x_sample, cache_k, cache_v, cache_logf, page_table, w_in, b_forget,
           lambda_q1, lambda_k1, lambda_q2, lambda_k2, g_subln, w_out, ln1_g, ln1_b,
           w_router, b_router, w_gate_up, w_down, ln2_g, ln2_b):
    batch, seq, _ = x_prompt.shape
    dec_b, dec_t, _ = x_sample.shape
    depth = w_in.shape[0]
    n_pool, page = cache_k.shape[1], cache_k.shape[2]
    n_pages = page_table.shape[1]
    past_len = n_pages * page
    alpha = (2 * depth) ** 0.25

    n_prompt = batch * seq
    n_sample = dec_b * dec_t
    n_real = n_prompt + n_sample
    n_tok = n_prompt + ROW_TILE
    n_prompt_tiles = n_prompt // ROW_TILE
    tiles_per_seq = seq // ROW_TILE
    assert seq % ATT_BLOCK == 0 or seq < ATT_BLOCK
    assert seq % ROW_TILE == 0 and n_sample <= ROW_TILE and dec_t == 4
    assert n_pages % PAGES_PER_STEP == 0 and page == LANES

    x = jnp.concatenate([x_prompt.reshape(n_prompt, D_MODEL), x_sample.reshape(n_sample, D_MODEL),
                         jnp.zeros((ROW_TILE - n_sample, D_MODEL), F32)], axis=0)

    pos_s = past_len + (jnp.arange(ROW_TILE) % dec_t)
    rope_c, rope_sa, rope_sb = _rope_tables(jnp.concatenate([jnp.arange(seq), pos_s]))

    u_incl = _tri(SB_KBLOCK, lambda r, c: r <= c)
    u_strict = _tri(SB_KBLOCK, lambda r, c: r > c)
    ut_strict = _tri(PAGES_PER_STEP * page, lambda r, c: r > c)
    low_strict = _tri(ROW_TILE, lambda r, c: c < r)

    qmask, qidx = _sample_col_layout()
    ncol = H_TOT * COLS_PER_HEAD
    colv = jnp.arange(ncol)
    et = ((colv[:, None] // COLS_PER_HEAD - H_A) == jnp.arange(LANES)[None, :]) \
        & (colv[:, None] >= H_A * COLS_PER_HEAD) & (colv[:, None] < (H_A + H_B) * COLS_PER_HEAD)
    et = et.astype(BF16)

    ck = cache_k.reshape(depth, n_pool, page, D_MODEL)
    cv = cache_v.reshape(depth, n_pool, page, D_MODEL)

    wr_pad = jnp.zeros((D_MODEL, LANES), F32).at[:, :N_EXPERTS].set(w_router)
    wrh = wr_pad.astype(BF16)
    wrl = (wr_pad - wrh.astype(F32)).astype(BF16)
    br = jnp.zeros((1, LANES), F32).at[0, :N_EXPERTS].set(b_router)

    n_tiles = (2 * n_real + N_EXPERTS * (ROW_TILE - 1)) // ROW_TILE + 1
    tok_ids = jnp.arange(n_tok, dtype=I32)

    kp, vp, fp, ksm, vsm, fsm = [], [], [], [], [], []
    for l in range(depth):
        lam_init = 0.8 - 0.6 * math.exp(-0.3 * l)
        scal = jnp.full((1,), lam_init, F32)
        wl = w_in[l]
        wq = wl[:, :D_MODEL].astype(BF16)
        wk = wl[:, D_MODEL:2 * D_MODEL].astype(BF16)
        wv = wl[:, 2 * D_MODEL:3 * D_MODEL].astype(BF16)
        wf = jnp.zeros((D_MODEL, LANES), F32).at[:, :H_B].set(wl[:, 3 * D_MODEL:]).astype(BF16)
        bfg = jnp.zeros((1, LANES), F32).at[0, :H_B].set(b_forget[l])
        lq1, lk1 = lambda_q1[l][None, :], lambda_k1[l][None, :]
        lq2, lk2 = lambda_q2[l][None, :], lambda_k2[l][None, :]
        gsub = g_subln[l][None, :]

        xb, lf, lft = _prep(x, wf, bfg)
        qb, kf, vf, kb, vb = _qkv(xb, wq, wk, wv, rope_c, rope_sa, rope_sb,
                                  n_prompt_tiles, tiles_per_seq)
        cumt = _cumsum(lft, u_incl, batch, seq)
        o_p = _attn(scal, qb, kb, vb, cumt, u_strict, lq1, lk1, lq2, lk2, gsub, batch, seq)

        q_s = qb[n_prompt:n_real].reshape(dec_b, dec_t, D_MODEL)
        qbd = jnp.where(qmask[None], jnp.swapaxes(q_s, 1, 2)[:, :, qidx], jnp.zeros((), BF16))
        pad = ((0, 0), (0, page - dec_t), (0, 0))
        kn = jnp.pad(kb[n_prompt:n_real].reshape(dec_b, dec_t, D_MODEL), pad)
        vn = jnp.pad(vb[n_prompt:n_real].reshape(dec_b, dec_t, D_MODEL), pad)
        lf_s = lf[n_prompt:n_real].reshape(dec_b, dec_t, LANES)
        lfn = jnp.einsum('cr,btr->bct', et.astype(F32), lf_s, precision=lax.Precision.HIGHEST)
        lfn = jnp.pad(lfn, ((0, 0), (0, 0), (0, page - dec_t)))
        o_s = _sattn(page_table, scal, l, ck, cv, cache_logf, qbd, kn, vn, lfn, et, ut_strict,
                     lq1, lk1, lq2, lk2, gsub)
        o_s = jnp.swapaxes(o_s[:, :, :dec_t, :], 1, 2).reshape(n_sample, D_MODEL).astype(BF16)
        o = jnp.concatenate([o_p, o_s, jnp.zeros((ROW_TILE - n_sample, D_MODEL), BF16)], axis=0)

        x1, logits = _oproj(o, w_out[l].astype(BF16), x, ln1_g[l][None, :], ln1_b[l][None, :],
                            wrh, wrl, br, alpha)
        route, cnt = _router(logits, low_strict, n_real)

        e12 = route[:, 0:2].astype(I32)
        rank = route[:, 4:6].astype(I32)
        counts = cnt[0, :N_EXPERTS].astype(I32)
        padded = ((counts + ROW_TILE - 1) // ROW_TILE) * ROW_TILE
        ends = jnp.cumsum(padded)
        starts = ends - padded
        valid = (tok_ids < n_real)[:, None]
        slot = jnp.where(valid, starts[e12] + rank, 0)
        tile_expert = jnp.minimum(
            jnp.searchsorted(ends, jnp.arange(n_tiles, dtype=I32) * ROW_TILE, side='right'),
            N_EXPERTS - 1).astype(I32)
        n_used = (ends[-1] // ROW_TILE).astype(I32).reshape(1)
        scatter_to = jnp.where(valid, slot, n_tiles * ROW_TILE).reshape(-1)
        tok_of_slot = jnp.zeros((n_tiles * ROW_TILE,), I32).at[scatter_to].set(
            jnp.repeat(tok_ids, 2), mode='drop')

        ys = _experts(tile_expert, n_used, tok_of_slot, x1,
                      w_gate_up[l].astype(BF16), w_down[l].astype(BF16), n_tiles)
        x = _combine(slot[:, 0], slot[:, 1], x1, route, ln2_g[l][None, :], ln2_b[l][None, :],
                     ys, alpha)

        kp.append(kf[:n_prompt].reshape(batch, seq, H_TOT, HEAD_DIM))
        vp.append(vf[:n_prompt].reshape(batch, seq, H_TOT, HEAD_DIM))
        fp.append(lf[:n_prompt, :H_B].reshape(batch, seq, H_B))
        ksm.append(kf[n_prompt:n_real].reshape(dec_b, dec_t, H_TOT, HEAD_DIM))
        vsm.append(vf[n_prompt:n_real].reshape(dec_b, dec_t, H_TOT, HEAD_DIM))
        fsm.append(lf[n_prompt:n_real, :H_B].reshape(dec_b, dec_t, H_B))

    y_p = x[:n_prompt].reshape(batch, seq, D_MODEL)
    y_s = x[n_prompt:n_real].reshape(dec_b, dec_t, D_MODEL)
    return (y_p, y_s, jnp.stack(kp), jnp.stack(vp), jnp.stack(fp),
            jnp.stack(ksm), jnp.stack(vsm), jnp.stack(fsm))
```

```python
import functools
import math

import jax
import jax.numpy as jnp
from jax import lax
from jax.experimental import pallas as pl
from jax.experimental.pallas import tpu as pltpu

F32 = jnp.float32
BF16 = jnp.bfloat16
I32 = jnp.int32

HEAD_DIM = 128
H_TOT = 16
H_A = 6
H_B = 5
H_C = 5
D_MODEL = H_TOT * HEAD_DIM
DQK_A = HEAD_DIM // 2
ROT_A = DQK_A // 4
ROPE_THETA = 500000.0
N_EXPERTS = 16
N_GROUPS = 4
EXPERTS_PER_GROUP = 4
D_FF = D_MODEL // 2
LN_EPS = 1e-5
NEG = -1e30

LANES = 128
ROW_TILE = 256
HEADS_PER_STEP = 4
ATT_BLOCK = 512
SB_KBLOCK = 256
PAGES_PER_STEP = 4
COLS_PER_HEAD = 8
GATHER_UNROLL = 8
VMEM_LIMIT = 56 * 1024 * 1024


def _dot(a, b):
    return jnp.dot(a, b, preferred_element_type=F32)


def _dot_nt(a, b):
    return lax.dot_general(a, b, (((1,), (1,)), ((), ())), preferred_element_type=F32)


def _log_sigmoid(x):
    return jnp.minimum(x, 0.0) - jnp.log1p(jnp.exp(-jnp.abs(x)))


def _log_sigmoid_scores(x):
    return jnp.minimum(x, 0.0) - jnp.log(1.0 + jnp.exp(-jnp.abs(x)))


def _split2(x):
    hi = x.astype(BF16)
    lo = (x - hi.astype(F32)).astype(BF16)
    return hi, lo


def _layer_norm(y, g, b):
    mu = jnp.mean(y, axis=-1, keepdims=True)
    d = y - mu
    var = jnp.mean(d * d, axis=-1, keepdims=True)
    return d * lax.rsqrt(var + LN_EPS) * g + b


def _prep_kernel(x_ref, wf_ref, bf_ref, xb_ref, lf_ref, lft_ref):
    xb = x_ref[...].astype(BF16)
    xb_ref[...] = xb
    lf = _log_sigmoid(_dot(xb, wf_ref[...]) + bf_ref[...])
    lf_ref[...] = lf
    lft_ref[...] = lf.T[:8, :]


def _prep(x, wf, bfg):
    n = x.shape[0]
    nt = n // ROW_TILE
    return pl.pallas_call(
        _prep_kernel,
        grid=(nt,),
        in_specs=[pl.BlockSpec((ROW_TILE, D_MODEL), lambda i: (i, 0)),
                  pl.BlockSpec((D_MODEL, LANES), lambda i: (0, 0)),
                  pl.BlockSpec((1, LANES), lambda i: (0, 0))],
        out_specs=[pl.BlockSpec((ROW_TILE, D_MODEL), lambda i: (i, 0)),
                   pl.BlockSpec((ROW_TILE, LANES), lambda i: (i, 0)),
                   pl.BlockSpec((8, ROW_TILE), lambda i: (0, i))],
        out_shape=[jax.ShapeDtypeStruct((n, D_MODEL), BF16),
                   jax.ShapeDtypeStruct((n, LANES), F32),
                   jax.ShapeDtypeStruct((8, n), F32)],
        compiler_params=pltpu.CompilerParams(dimension_semantics=("parallel",)),
        name="prep",
    )(x, wf, bfg)


def _qkv_kernel(xb_ref, wq_ref, wk_ref, wv_ref, c_ref, sa_ref, sb_ref,
                q_ref, k_ref, v_ref, kb_ref, vb_ref, wb_scr):
    j = pl.program_id(0)

    @pl.when(pl.program_id(1) == 0)
    def _():
        wb_scr[0] = wq_ref[0].astype(BF16)
        wb_scr[1] = wk_ref[0].astype(BF16)
        wb_scr[2] = wv_ref[0].astype(BF16)

    xb = xb_ref[...]
    q = _dot(xb, wb_scr[0])
    k = _dot(xb, wb_scr[1])
    v = _dot(xb, wb_scr[2])
    v_ref[...] = v
    vb_ref[...] = v.astype(BF16)
    scale_a = DQK_A ** -0.5
    scale_bc = HEAD_DIM ** -0.5

    def rope(t):
        return (t * c_ref[...] + pltpu.roll(t, LANES - ROT_A // 2, 1) * sa_ref[...]
                + pltpu.roll(t, ROT_A // 2, 1) * sb_ref[...])

    @pl.when(j * HEADS_PER_STEP < H_A)
    def _():
        for hh in range(HEADS_PER_STEP):
            sl = slice(hh * HEAD_DIM, (hh + 1) * HEAD_DIM)
            use = (j * HEADS_PER_STEP + hh) < H_A
            qh = q[:, sl]
            kh = k[:, sl]
            qh = jnp.where(use, rope(qh), qh)
            kh = jnp.where(use, rope(kh), kh)
            scale = jnp.where(use, scale_a, scale_bc)
            q_ref[:, sl] = (qh * scale).astype(BF16)
            k_ref[:, sl] = kh
            kb_ref[:, sl] = kh.astype(BF16)

    @pl.when(j * HEADS_PER_STEP >= H_A)
    def _():
        q_ref[...] = (q * scale_bc).astype(BF16)
        k_ref[...] = k
        kb_ref[...] = k.astype(BF16)


def _qkv(xb, w_in, layer, rope_c, rope_sa, rope_sb, n_prompt_tiles, tiles_per_seq):
    n = xb.shape[0]
    nt = n // ROW_TILE
    wcol = HEADS_PER_STEP * HEAD_DIM
    nj = D_MODEL // wcol

    def tbl(j, i):
        return (jnp.where(i < n_prompt_tiles, i % tiles_per_seq, tiles_per_seq), 0)

    def wspec(part):
        return pl.BlockSpec((1, D_MODEL, wcol), lambda j, i: (layer, 0, part * nj + j))

    ospec = pl.BlockSpec((ROW_TILE, wcol), lambda j, i: (i, j))
    tspec = pl.BlockSpec((ROW_TILE, LANES), tbl)
    return pl.pallas_call(
        _qkv_kernel,
        grid=(nj, nt),
        in_specs=[pl.BlockSpec((ROW_TILE, D_MODEL), lambda j, i: (i, 0)),
                  wspec(0), wspec(1), wspec(2), tspec, tspec, tspec],
        out_specs=[ospec, ospec, ospec, ospec, ospec],
        scratch_shapes=[pltpu.VMEM((3, D_MODEL, wcol), BF16)],
        out_shape=[jax.ShapeDtypeStruct((n, D_MODEL), BF16),
                   jax.ShapeDtypeStruct((n, D_MODEL), F32),
                   jax.ShapeDtypeStruct((n, D_MODEL), F32),
                   jax.ShapeDtypeStruct((n, D_MODEL), BF16),
                   jax.ShapeDtypeStruct((n, D_MODEL), BF16)],
        compiler_params=pltpu.CompilerParams(dimension_semantics=("parallel", "arbitrary"),
                                             vmem_limit_bytes=VMEM_LIMIT),
        name="qkv",
    )(xb, w_in, w_in, w_in, rope_c, rope_sa, rope_sb)


def _cumsum_kernel(lft_ref, u_ref, out_ref):
    t = lft_ref.shape[1]
    blk = u_ref.shape[0]
    u = u_ref[...]
    carry = jnp.zeros((8, 1), F32)
    for b in range(t // blk):
        x = lft_ref[:, b * blk:(b + 1) * blk]
        hi = x.astype(BF16)
        r = x - hi.astype(F32)
        mid = r.astype(BF16)
        lo = (r - mid.astype(F32)).astype(BF16)
        cs = _dot(hi, u) + _dot(mid, u) + _dot(lo, u) + carry
        out_ref[:, b * blk:(b + 1) * blk] = cs
        carry = cs[:, blk - 1:blk]


def _cumsum(lft, u_incl, batch, seq):
    return pl.pallas_call(
        _cumsum_kernel,
        grid=(batch,),
        in_specs=[pl.BlockSpec((8, seq), lambda b: (0, b)),
                  pl.BlockSpec(u_incl.shape, lambda b: (0, 0))],
        out_specs=pl.BlockSpec((8, seq), lambda b: (0, b)),
        out_shape=jax.ShapeDtypeStruct((8, batch * seq), F32),
        compiler_params=pltpu.CompilerParams(dimension_semantics=("parallel",)),
        name="cumsum",
    )(lft, u_incl)


def _softmax_update(s, m, l, acc, v):
    m_new = jnp.maximum(m, jnp.max(s, axis=1, keepdims=True))
    alpha = jnp.exp(m - m_new)
    p = jnp.exp(s - m_new)
    l = alpha * l + jnp.sum(p, axis=1, keepdims=True)
    acc = alpha * acc + _dot(p.astype(BF16), v)
    return m_new, l, acc


def _attn_kernel(scal_ref, q_ref, k_ref, v_ref, cum_ref, u_ref,
                 lq1_ref, lk1_ref, lq2_ref, lk2_ref, g_ref, o_ref, *, tq, tkc):
    h = pl.program_id(1)
    qi = pl.program_id(2)
    lam_init = scal_ref[0]

    def kv(kb, width):
        start = pl.multiple_of(kb * width, width)
        return k_ref[pl.ds(start, width), :], v_ref[pl.ds(start, width), :]

    def causal(width_k, kb, strict):
        row = qi * tq + lax.broadcasted_iota(I32, (tq, width_k), 0)
        col = kb * width_k + lax.broadcasted_iota(I32, (tq, width_k), 1)
        return (col < row) if strict else (col <= row)

    def init():
        return (jnp.full((tq, 1), NEG, F32), jnp.zeros((tq, 1), F32),
                jnp.zeros((tq, HEAD_DIM), F32))

    @pl.when(h < H_A)
    def _():
        q = q_ref[...]
        lane = lax.broadcasted_iota(I32, q.shape, 1)
        zero = jnp.zeros_like(q)
        q0 = jnp.where(lane < DQK_A, q, zero)
        q1 = jnp.where(lane >= DQK_A, q, zero)

        def step(kb, carry, masked):
            c0, c1 = carry
            k, v = kv(kb, tq)
            s0 = _dot_nt(q0, k)
            s1 = _dot_nt(q1, k)
            if masked:
                ok = causal(tq, kb, False)
                s0 = jnp.where(ok, s0, NEG)
                s1 = jnp.where(ok, s1, NEG)
            return _softmax_update(s0, *c0, v), _softmax_update(s1, *c1, v)

        carry = lax.fori_loop(0, qi, lambda kb, c: step(kb, c, False), (init(), init()))
        (_, l0, a0), (_, l1, a1) = step(qi, carry, True)
        lam = (jnp.exp(jnp.sum(lq1_ref[...] * lk1_ref[...], axis=1, keepdims=True))
               - jnp.exp(jnp.sum(lq2_ref[...] * lk2_ref[...], axis=1, keepdims=True)) + lam_init)
        o = a0 / l0 - lam * (a1 / l1)
        o = o * lax.rsqrt(jnp.mean(o * o, axis=1, keepdims=True) + LN_EPS) * g_ref[...]
        o_ref[...] = (o * (1.0 - lam_init)).astype(o_ref.dtype)

    @pl.when((h >= H_A) & (h < H_A + H_B))
    def _():
        q = q_ref[...]
        hb = h - H_A

        def step(kb, carry, masked):
            k, v = kv(kb, tq)
            start = pl.multiple_of(kb * tq, tq)
            s = _dot_nt(q, k) - cum_ref[pl.ds(hb, 1), pl.ds(start, tq)]
            if masked:
                s = jnp.where(causal(tq, kb, False), s, NEG)
            return _softmax_update(s, *carry, v)

        carry = lax.fori_loop(0, qi, lambda kb, c: step(kb, c, False), init())
        _, l, acc = step(qi, carry, True)
        o_ref[...] = (acc / l).astype(o_ref.dtype)

    @pl.when(h >= H_A + H_B)
    def _():
        q = q_ref[...]
        u = u_ref[...]
        ratio = tq // tkc

        def step(kb, carry, masked):
            run, acc = carry
            k, v = kv(kb, tkc)
            z = _dot_nt(q, k)
            lsz = _log_sigmoid_scores(z)
            ls = lsz - z
            if masked:
                ok = causal(tkc, kb, True)
                ls = jnp.where(ok, ls, 0.0)
            hi, lo = _split2(ls)
            between = _dot(hi, u) + _dot(lo, u) + run
            w = jnp.exp(lsz + between)
            if masked:
                w = jnp.where(ok, w, 0.0)
            acc = acc + _dot(w.astype(BF16), v)
            run = run + jnp.sum(ls, axis=1, keepdims=True)
            return run, acc

        carry = (jnp.zeros((tq, 1), F32), jnp.zeros((tq, HEAD_DIM), F32))
        for d in range(ratio):
            carry = step(qi * ratio + (ratio - 1 - d), carry, True)
        nfull = qi * ratio
        carry = lax.fori_loop(0, nfull, lambda i, c: step(nfull - 1 - i, c, False), carry)
        o_ref[...] = carry[1].astype(o_ref.dtype)


def _attn(scal, qb, kb, vb, cumt, u_strict, lq1, lk1, lq2, lk2, g, batch, seq):
    tq = min(ATT_BLOCK, seq)
    tkc = min(SB_KBLOCK, tq)
    nq = seq // tq
    lspec = pl.BlockSpec((1, DQK_A), lambda b, h, i: (0, 0))
    return pl.pallas_call(
        functools.partial(_attn_kernel, tq=tq, tkc=tkc),
        grid=(batch, H_TOT, nq),
        in_specs=[pl.BlockSpec(memory_space=pltpu.SMEM),
                  pl.BlockSpec((tq, HEAD_DIM), lambda b, h, i: (b * nq + i, h)),
                  pl.BlockSpec((seq, HEAD_DIM), lambda b, h, i: (b, h)),
                  pl.BlockSpec((seq, HEAD_DIM), lambda b, h, i: (b, h)),
                  pl.BlockSpec((8, seq), lambda b, h, i: (0, b)),
                  pl.BlockSpec((tkc, tkc), lambda b, h, i: (0, 0)),
                  lspec, lspec, lspec, lspec,
                  pl.BlockSpec((1, HEAD_DIM), lambda b, h, i: (0, 0))],
        out_specs=pl.BlockSpec((tq, HEAD_DIM), lambda b, h, i: (b * nq + i, h)),
        out_shape=jax.ShapeDtypeStruct((batch * seq, D_MODEL), BF16),
        compiler_params=pltpu.CompilerParams(
            dimension_semantics=("parallel", "parallel", "parallel"),
            vmem_limit_bytes=VMEM_LIMIT),
        name="attn",
    )(scal, qb, kb, vb, cumt, u_strict, lq1, lk1, lq2, lk2, g)


def _sattn_kernel(*refs, n_pages_step, page):
    g_n = n_pages_step
    pt_ref, scal_ref = refs[0], refs[1]
    k_refs = refs[2:2 + g_n]
    v_refs = refs[2 + g_n:2 + 2 * g_n]
    f_refs = refs[2 + 2 * g_n:2 + 3 * g_n]
    (qbd_ref, kn_ref, vn_ref, lfn_ref, ut_ref,
     lq1_ref, lk1_ref, lq2_ref, lk2_ref, gs_ref,
     o_ref, m_scr, l_scr, run_scr, acc_scr, kb_scr, vb_scr, st_scr, lft_scr) = refs[2 + 3 * g_n:]
    del pt_ref
    t = pl.program_id(1)
    nsteps = pl.num_programs(1)
    lam_init = scal_ref[0]
    ncol = H_TOT * COLS_PER_HEAD
    col_a = H_A * COLS_PER_HEAD
    col_b = (H_A + H_B) * COLS_PER_HEAD

    def process(width, valid_ab, valid_c):
        st = st_scr[:, :width]
        lft = lft_scr[:, :width]
        row = lax.broadcasted_iota(I32, (ncol, width), 0)
        is_a = row < col_a
        is_c = row >= col_b
        lsz = _log_sigmoid_scores(st)
        ls = lsz - st
        if valid_c is not None:
            ls = jnp.where(valid_c, ls, 0.0)
        x = jnp.where(is_c, ls, lft)
        hi, lo = _split2(x)
        ut = ut_ref[:width, :width]
        suffix = _dot(hi, ut) + _dot(lo, ut) + run_scr[...]
        logit = jnp.where(is_a, st, jnp.where(is_c, lsz, st) + suffix)
        if valid_ab is not None:
            logit = jnp.where((is_c & valid_c) | (~is_c & valid_ab), logit, NEG)
        m_old = m_scr[...]
        m_new = jnp.where(is_c[:, :1], 0.0,
                          jnp.maximum(m_old, jnp.max(logit, axis=1, keepdims=True)))
        p = jnp.exp(logit - m_new)
        alpha = jnp.exp(m_old - m_new)
        l_scr[...] = alpha * l_scr[...] + jnp.sum(p, axis=1, keepdims=True)
        m_scr[...] = m_new
        run_scr[...] = run_scr[...] + jnp.sum(x, axis=1, keepdims=True)
        pb = p.astype(BF16)
        for hp in range(H_TOT // 2):
            rs = slice(2 * COLS_PER_HEAD * hp, 2 * COLS_PER_HEAD * (hp + 1))
            cs = slice(2 * HEAD_DIM * hp, 2 * HEAD_DIM * (hp + 1))
            acc_scr[rs, :] = alpha[rs, :] * acc_scr[rs, :] + _dot(pb[rs, :], vb_scr[:width, cs])

    @pl.when(t == 0)
    def _():
        row = lax.broadcasted_iota(I32, (ncol, 1), 0)
        m_scr[...] = jnp.where(row >= col_b, 0.0, NEG)
        l_scr[...] = jnp.zeros_like(l_scr)
        run_scr[...] = jnp.zeros_like(run_scr)
        acc_scr[...] = jnp.zeros_like(acc_scr)
        lft_scr[...] = jnp.zeros_like(lft_scr)
        st_scr[:, :page] = _dot(kn_ref[0], qbd_ref[0]).T
        lft_scr[:, :page] = lfn_ref[0]
        vb_scr[:page, :] = vn_ref[0]
        qidx = lax.broadcasted_iota(I32, (ncol, page), 0) % 4
        key = lax.broadcasted_iota(I32, (ncol, page), 1)
        process(page, key <= qidx, key < qidx)

    for g in range(g_n):
        sl = slice(g * page, (g + 1) * page)
        for h in range(H_TOT):
            cs = slice(h * HEAD_DIM, (h + 1) * HEAD_DIM)
            kb_scr[sl, cs] = k_refs[g][0, 0, pl.ds(h, page, stride=H_TOT), :].astype(BF16)
            vb_scr[sl, cs] = v_refs[g][0, 0, pl.ds(h, page, stride=H_TOT), :].astype(BF16)
        st_scr[:, sl] = _dot(kb_scr[sl, :], qbd_ref[0]).T
        for hb in range(H_B):
            r0 = (H_A + hb) * COLS_PER_HEAD
            lft_scr[r0:r0 + COLS_PER_HEAD, sl] = jnp.broadcast_to(
                f_refs[g][0, 0, hb:hb + 1, :], (COLS_PER_HEAD, page))
    process(g_n * page, None, None)

    @pl.when(t == nsteps - 1)
    def _():
        row = lax.broadcasted_iota(I32, (ncol, 1), 0)
        linv = jnp.where(row >= col_b, 1.0, 1.0 / l_scr[...])
        lam = (jnp.exp(jnp.sum(lq1_ref[...] * lk1_ref[...], axis=1, keepdims=True))
               - jnp.exp(jnp.sum(lq2_ref[...] * lk2_ref[...], axis=1, keepdims=True)) + lam_init)
        half = COLS_PER_HEAD // 2
        for h in range(H_TOT):
            rs = slice(COLS_PER_HEAD * h, COLS_PER_HEAD * (h + 1))
            cs = slice(HEAD_DIM * (h % 2), HEAD_DIM * (h % 2 + 1))
            blk = acc_scr[rs, cs] * linv[rs, :]
            if h < H_A:
                o = blk - lam * pltpu.roll(blk, half, 0)
                o = o * lax.rsqrt(jnp.mean(o * o, axis=1, keepdims=True) + LN_EPS) * gs_ref[...]
                blk = o * (1.0 - lam_init)
            o_ref[0, h] = blk


def _sattn(page_table, scal, layer, ck, cv, cft, qbd, kn, vn, lfn, ut_strict,
           lq1, lk1, lq2, lk2, g):
    nb, n_pages = page_table.shape
    page = cft.shape[3]
    g_n = PAGES_PER_STEP
    nsteps = n_pages // g_n
    ncol = H_TOT * COLS_PER_HEAD

    def page_map(gg):
        def f(b, t, pt):
            return (layer, pt[b, (nsteps - 1 - t) * g_n + gg], 0, 0)
        return f

    kspecs = [pl.BlockSpec((1, 1, page * H_TOT, HEAD_DIM), page_map(gg)) for gg in range(g_n)]
    fspecs = [pl.BlockSpec((1, 1, 8, page), page_map(gg)) for gg in range(g_n)]
    per_b3 = lambda b, t, pt: (b, 0, 0)
    const2 = lambda b, t, pt: (0, 0)
    lspec = pl.BlockSpec((1, DQK_A), const2)
    in_specs = ([pl.BlockSpec(memory_space=pltpu.SMEM)] + kspecs + kspecs + fspecs + [
        pl.BlockSpec((1, D_MODEL, ncol), per_b3),
        pl.BlockSpec((1, page, D_MODEL), per_b3),
        pl.BlockSpec((1, page, D_MODEL), per_b3),
        pl.BlockSpec((1, ncol, page), per_b3),
        pl.BlockSpec(ut_strict.shape, const2),
        lspec, lspec, lspec, lspec,
        pl.BlockSpec((1, HEAD_DIM), const2)])
    width = g_n * page
    grid_spec = pltpu.PrefetchScalarGridSpec(
        num_scalar_prefetch=1,
        grid=(nb, nsteps),
        in_specs=in_specs,
        out_specs=pl.BlockSpec((1, H_TOT, COLS_PER_HEAD, HEAD_DIM), lambda b, t, pt: (b, 0, 0, 0)),
        scratch_shapes=[pltpu.VMEM((ncol, 1), F32), pltpu.VMEM((ncol, 1), F32),
                        pltpu.VMEM((ncol, 1), F32), pltpu.VMEM((ncol, 2 * HEAD_DIM), F32),
                        pltpu.VMEM((width, D_MODEL), BF16), pltpu.VMEM((width, D_MODEL), BF16),
                        pltpu.VMEM((ncol, width), F32), pltpu.VMEM((ncol, width), F32)])
    return pl.pallas_call(
        functools.partial(_sattn_kernel, n_pages_step=g_n, page=page),
        grid_spec=grid_spec,
        out_shape=jax.ShapeDtypeStruct((nb, H_TOT, COLS_PER_HEAD, HEAD_DIM), F32),
        compiler_params=pltpu.CompilerParams(dimension_semantics=("parallel", "arbitrary"),
                                             vmem_limit_bytes=VMEM_LIMIT),
        name="sattn",
    )(page_table, scal, *([ck] * g_n), *([cv] * g_n), *([cft] * g_n),
      qbd, kn, vn, lfn, ut_strict, lq1, lk1, lq2, lk2, g)


def _oproj_kernel(op_ref, os_ref, wo_ref, x_ref, g_ref, b_ref, wrh_ref, wrl_ref, br_ref,
                  x1_ref, lg_ref, *, alpha, n_prompt_tiles):
    o = jnp.where(pl.program_id(0) < n_prompt_tiles, op_ref[...], os_ref[...])
    mix = _dot(o, wo_ref[0])
    x1 = _layer_norm(alpha * x_ref[...] + mix, g_ref[...], b_ref[...])
    x1_ref[...] = x1
    xh, xl = _split2(x1)
    wrh = wrh_ref[...]
    lg_ref[...] = _dot(xh, wrh) + _dot(xl, wrh) + _dot(xh, wrl_ref[...]) + br_ref[...]


def _oproj(o_p, o_s, wo, layer, x, g, b, wrh, wrl, br, alpha):
    n = x.shape[0]
    nt = n // ROW_TILE
    n_prompt_tiles = o_p.shape[0] // ROW_TILE
    row = lambda i: (i, 0)
    const = lambda i: (0, 0)
    return pl.pallas_call(
        functools.partial(_oproj_kernel, alpha=alpha, n_prompt_tiles=n_prompt_tiles),
        grid=(nt,),
        in_specs=[pl.BlockSpec((ROW_TILE, D_MODEL), lambda i: (jnp.minimum(i, n_prompt_tiles - 1), 0)),
                  pl.BlockSpec((ROW_TILE, D_MODEL), const),
                  pl.BlockSpec((1, D_MODEL, D_MODEL), lambda i: (layer, 0, 0)),
                  pl.BlockSpec((ROW_TILE, D_MODEL), row),
                  pl.BlockSpec((1, D_MODEL), const),
                  pl.BlockSpec((1, D_MODEL), const),
                  pl.BlockSpec((D_MODEL, LANES), const),
                  pl.BlockSpec((D_MODEL, LANES), const),
                  pl.BlockSpec((1, LANES), const)],
        out_specs=[pl.BlockSpec((ROW_TILE, D_MODEL), row),
                   pl.BlockSpec((ROW_TILE, LANES), row)],
        out_shape=[jax.ShapeDtypeStruct((n, D_MODEL), F32),
                   jax.ShapeDtypeStruct((n, LANES), F32)],
        compiler_params=pltpu.CompilerParams(dimension_semantics=("parallel",),
                                             vmem_limit_bytes=VMEM_LIMIT),
        name="oproj",
    )(o_p, o_s, wo, x, g, b, wrh, wrl, br)


def _router_kernel(lg_ref, low_ref, route_ref, cnt_ref, run_scr, *, n_real):
    i = pl.program_id(0)
    tm = lg_ref.shape[0]

    @pl.when(i == 0)
    def _():
        run_scr[...] = jnp.zeros_like(run_scr)

    lg = lg_ref[...]
    lane = lax.broadcasted_iota(I32, lg.shape, 1)
    lane_f = lane.astype(F32)
    tok = i * tm + lax.broadcasted_iota(I32, lg.shape, 0)
    is_e = lane < N_EXPERTS
    lgm = jnp.where(is_e, lg, NEG)
    ex = jnp.where(is_e, jnp.exp(lgm - jnp.max(lgm, axis=1, keepdims=True)), 0.0)
    probs = ex / jnp.sum(ex, axis=1, keepdims=True)

    best = None
    for grp in range(N_GROUPS):
        ing = (lane >= grp * EXPERTS_PER_GROUP) & (lane < (grp + 1) * EXPERTS_PER_GROUP)
        pg = jnp.where(ing, probs, -1.0)
        p1 = jnp.max(pg, axis=1, keepdims=True)
        i1 = jnp.min(jnp.where(pg == p1, lane_f, 1e9), axis=1, keepdims=True)
        pg2 = jnp.where(lane_f == i1, -1.0, pg)
        p2 = jnp.max(pg2, axis=1, keepdims=True)
        i2 = jnp.min(jnp.where(pg2 == p2, lane_f, 1e9), axis=1, keepdims=True)
        cand = (p1 + p2, p1, i1, p2, i2)
        if best is None:
            best = cand
        else:
            better = cand[0] > best[0]
            best = tuple(jnp.where(better, c, o) for c, o in zip(cand, best))
    _, p1, i1, p2, i2 = best
    den = p1 + p2
    g1 = p1 / den
    g2 = p2 / den

    sel1 = lane_f == i1
    sel2 = lane_f == i2
    oh = jnp.where((sel1 | sel2) & (tok < n_real), 1.0, 0.0)
    total = _dot(low_ref[...], oh.astype(BF16)) + run_scr[...]
    r1 = jnp.sum(jnp.where(sel1, total, 0.0), axis=1, keepdims=True)
    r2 = jnp.sum(jnp.where(sel2, total, 0.0), axis=1, keepdims=True)
    run = run_scr[...] + jnp.sum(oh, axis=0, keepdims=True)
    run_scr[...] = run
    cnt_ref[...] = jnp.broadcast_to(run, cnt_ref.shape)
    route = jnp.zeros_like(lg)
    for pos, val in enumerate((i1, i2, g1, g2, r1, r2)):
        route = jnp.where(lane == pos, val, route)
    route_ref[...] = route


def _router(logits, low_strict, n_real):
    n = logits.shape[0]
    nt = n // ROW_TILE
    return pl.pallas_call(
        functools.partial(_router_kernel, n_real=n_real),
        grid=(nt,),
        in_specs=[pl.BlockSpec((ROW_TILE, LANES), lambda i: (i, 0)),
                  pl.BlockSpec((ROW_TILE, ROW_TILE), lambda i: (0, 0))],
        out_specs=[pl.BlockSpec((ROW_TILE, LANES), lambda i: (i, 0)),
                   pl.BlockSpec((8, LANES), lambda i: (0, 0))],
        out_shape=[jax.ShapeDtypeStruct((n, LANES), F32),
                   jax.ShapeDtypeStruct((8, LANES), F32)],
        scratch_shapes=[pltpu.VMEM((1, LANES), F32)],
        compiler_params=pltpu.CompilerParams(dimension_semantics=("arbitrary",)),
        name="router",
    )(logits, low_strict)


def _row_gather_start(src_hbm, dst, sem, idx_ref, base, n_rows):
    def body(r, c):
        pltpu.make_async_copy(src_hbm.at[pl.ds(idx_ref[base + r], 1), :],
                              dst.at[pl.ds(r, 1), :], sem).start()
        return c
    lax.fori_loop(0, n_rows, body, 0, unroll=GATHER_UNROLL)


def _row_gather_wait(src_hbm, dst, sem, n_rows):
    def body(r, c):
        pltpu.make_async_copy(src_hbm.at[pl.ds(0, 1), :], dst.at[pl.ds(r, 1), :], sem).wait()
        return c
    lax.fori_loop(0, n_rows, body, 0, unroll=GATHER_UNROLL)


def _experts_kernel(te_ref, nu_ref, tok_ref, x_hbm, wgu_ref, wd_ref, ys_ref, xbuf, sem):
    del te_ref
    j = pl.program_id(0)
    nt = pl.num_programs(0)
    tm = ys_ref.shape[0]
    n_used = nu_ref[0]
    slot = j % 2

    @pl.when(j == 0)
    def _():
        _row_gather_start(x_hbm, xbuf.at[0], sem.at[0], tok_ref, 0, tm)

    @pl.when((j + 1 < nt) & (j + 1 < n_used))
    def _():
        _row_gather_start(x_hbm, xbuf.at[1 - slot], sem.at[1 - slot], tok_ref, (j + 1) * tm, tm)

    @pl.when(j < n_used)
    def _():
        _row_gather_wait(x_hbm, xbuf.at[slot], sem.at[slot], tm)
        xb = xbuf[slot].astype(BF16)
        hgu = _dot(xb, wgu_ref[0, 0])
        gate = hgu[:, :D_FF]
        act = gate * jax.nn.sigmoid(gate) * hgu[:, D_FF:]
        ys_ref[...] = _dot(act.astype(BF16), wd_ref[0, 0])

    @pl.when(j >= n_used)
    def _():
        ys_ref[...] = jnp.zeros_like(ys_ref)


def _experts(tile_expert, n_used, tok_of_slot, x1, wgu, wd, layer, n_tiles):
    grid_spec = pltpu.PrefetchScalarGridSpec(
        num_scalar_prefetch=3,
        grid=(n_tiles,),
        in_specs=[pl.BlockSpec(memory_space=pl.ANY),
                  pl.BlockSpec((1, 1, D_MODEL, 2 * D_FF),
                               lambda j, te, nu, tk: (layer, te[j], 0, 0)),
                  pl.BlockSpec((1, 1, D_FF, D_MODEL),
                               lambda j, te, nu, tk: (layer, te[j], 0, 0))],
        out_specs=pl.BlockSpec((ROW_TILE, D_MODEL), lambda j, te, nu, tk: (j, 0)),
        scratch_shapes=[pltpu.VMEM((2, ROW_TILE, D_MODEL), F32),
                        pltpu.SemaphoreType.DMA((2,))])
    return pl.pallas_call(
        _experts_kernel,
        grid_spec=grid_spec,
        out_shape=jax.ShapeDtypeStruct((n_tiles * ROW_TILE, D_MODEL), F32),
        compiler_params=pltpu.CompilerParams(dimension_semantics=("arbitrary",),
                                             vmem_limit_bytes=VMEM_LIMIT),
        name="experts",
    )(tile_expert, n_used, tok_of_slot, x1, wgu, wd)


def _combine_kernel(s1_ref, s2_ref, x1_ref, route_ref, g_ref, b_ref, ys_hbm, out_ref,
                    ybuf, sem, *, alpha):
    i = pl.program_id(0)
    nt = pl.num_programs(0)
    tm = out_ref.shape[0]
    slot = i % 2

    def start(tile, s):
        _row_gather_start(ys_hbm, ybuf.at[s, 0], sem.at[s, 0], s1_ref, tile * tm, tm)
        _row_gather_start(ys_hbm, ybuf.at[s, 1], sem.at[s, 1], s2_ref, tile * tm, tm)

    @pl.when(i == 0)
    def _():
        start(0, 0)

    @pl.when(i + 1 < nt)
    def _():
        start(i + 1, 1 - slot)

    route = route_ref[...]
    g1 = route[:, 2:3]
    g2 = route[:, 3:4]
    _row_gather_wait(ys_hbm, ybuf.at[slot, 0], sem.at[slot, 0], tm)
    _row_gather_wait(ys_hbm, ybuf.at[slot, 1], sem.at[slot, 1], tm)
    y = g1 * ybuf[slot, 0] + g2 * ybuf[slot, 1]
    out_ref[...] = _layer_norm(alpha * x1_ref[...] + y, g_ref[...], b_ref[...])


def _combine(slot1, slot2, x1, route, g, b, ys, alpha):
    n = x1.shape[0]
    nt = n // ROW_TILE
    row = lambda i, s1, s2: (i, 0)
    const = lambda i, s1, s2: (0, 0)
    grid_spec = pltpu.PrefetchScalarGridSpec(
        num_scalar_prefetch=2,
        grid=(nt,),
        in_specs=[pl.BlockSpec((ROW_TILE, D_MODEL), row),
                  pl.BlockSpec((ROW_TILE, LANES), row),
                  pl.BlockSpec((1, D_MODEL), const),
                  pl.BlockSpec((1, D_MODEL), const),
                  pl.BlockSpec(memory_space=pl.ANY)],
        out_specs=pl.BlockSpec((ROW_TILE, D_MODEL), row),
        scratch_shapes=[pltpu.VMEM((2, 2, ROW_TILE, D_MODEL), F32),
                        pltpu.SemaphoreType.DMA((2, 2))])
    return pl.pallas_call(
        functools.partial(_combine_kernel, alpha=alpha),
        grid_spec=grid_spec,
        out_shape=jax.ShapeDtypeStruct((n, D_MODEL), F32),
        compiler_params=pltpu.CompilerParams(dimension_semantics=("arbitrary",),
                                             vmem_limit_bytes=VMEM_LIMIT),
        name="combine",
    )(slot1, slot2, x1, route, g, b, ys)


def _rope_tables(pos):
    half = ROT_A // 2
    inv_freq = ROPE_THETA ** (-jnp.arange(half, dtype=F32) * 2.0 / ROT_A)
    ang = pos.astype(F32)[:, None] * inv_freq[None, :]
    cos, sin = jnp.cos(ang), jnp.sin(ang)
    lane = jnp.arange(LANES)
    m = lane % half
    in_x1 = (lane % DQK_A) < half
    in_x2 = ((lane % DQK_A) >= half) & ((lane % DQK_A) < ROT_A)
    c = jnp.where((in_x1 | in_x2)[None, :], cos[:, m], 1.0)
    sa = jnp.where(in_x1[None, :], -sin[:, m], 0.0)
    sb = jnp.where(in_x2[None, :], sin[:, m], 0.0)
    return c, sa, sb


def _tri(n, fn):
    r = jnp.arange(n)[:, None]
    c = jnp.arange(n)[None, :]
    return fn(r, c).astype(BF16)


def _sample_col_layout():
    ncol = H_TOT * COLS_PER_HEAD
    col = jnp.arange(ncol)
    head = col // COLS_PER_HEAD
    r = col % COLS_PER_HEAD
    is_a = head < H_A
    used = is_a | (r < COLS_PER_HEAD // 2)
    lo = head * HEAD_DIM + jnp.where(is_a, (r // 4) * DQK_A, 0)
    hi = lo + jnp.where(is_a, DQK_A, HEAD_DIM)
    feat = jnp.arange(D_MODEL)[:, None]
    mask = (feat >= lo[None, :]) & (feat < hi[None, :]) & used[None, :]
    return mask, r % 4


def kernel(x_prompt, x_sample, cache_k, cache_v, cache_logf, page_table, w_in, b_forget,
           lambda_q1, lambda_k1, lambda_q2, lambda_k2, g_subln, w_out, ln1_g, ln1_b,
           w_router, b_router, w_gate_up, w_down, ln2_g, ln2_b):
    batch, seq, _ = x_prompt.shape
    dec_b, dec_t, _ = x_sample.shape
    depth = w_in.shape[0]
    n_pool, page = cache_k.shape[1], cache_k.shape[2]
    n_pages = page_table.shape[1]
    past_len = n_pages * page
    alpha = (2 * depth) ** 0.25

    n_prompt = batch * seq
    n_sample = dec_b * dec_t
    n_real = n_prompt + n_sample
    n_tok = n_prompt + ROW_TILE
    n_prompt_tiles = n_prompt // ROW_TILE
    tiles_per_seq = seq // ROW_TILE
    assert seq % ATT_BLOCK == 0 or seq < ATT_BLOCK
    assert seq % ROW_TILE == 0 and n_sample <= ROW_TILE and dec_t == 4
    assert n_pages % PAGES_PER_STEP == 0 and page == LANES

    x = jnp.concatenate([x_prompt.reshape(n_prompt, D_MODEL), x_sample.reshape(n_sample, D_MODEL),
                         jnp.zeros((ROW_TILE - n_sample, D_MODEL), F32)], axis=0)

    pos_s = past_len + (jnp.arange(ROW_TILE) % dec_t)
    rope_c, rope_sa, rope_sb = _rope_tables(jnp.concatenate([jnp.arange(seq), pos_s]))

    u_incl = _tri(SB_KBLOCK, lambda r, c: r <= c)
    u_strict = _tri(SB_KBLOCK, lambda r, c: r > c)
    ut_strict = _tri(PAGES_PER_STEP * page, lambda r, c: r > c)
    low_strict = _tri(ROW_TILE, lambda r, c: c < r)

    qmask, qidx = _sample_col_layout()
    ncol = H_TOT * COLS_PER_HEAD
    col_head = jnp.arange(ncol) // COLS_PER_HEAD
    col_is_b = (col_head >= H_A) & (col_head < H_A + H_B)
    col_hb = jnp.clip(col_head - H_A, 0, H_B - 1)

    ck = cache_k.reshape(depth, n_pool, page * H_TOT, HEAD_DIM)
    cv = cache_v.reshape(depth, n_pool, page * H_TOT, HEAD_DIM)
    cft = jnp.pad(jnp.swapaxes(cache_logf, 2, 3), ((0, 0), (0, 0), (0, 8 - H_B), (0, 0)))

    wr_pad = jnp.zeros((D_MODEL, LANES), F32).at[:, :N_EXPERTS].set(w_router)
    wrh = wr_pad.astype(BF16)
    wrl = (wr_pad - wrh.astype(F32)).astype(BF16)
    br = jnp.zeros((1, LANES), F32).at[0, :N_EXPERTS].set(b_router)

    wo_b = w_out.astype(BF16)
    wgu_b = w_gate_up.astype(BF16)
    wd_b = w_down.astype(BF16)

    n_tiles = (2 * n_real + N_EXPERTS * (ROW_TILE - 1)) // ROW_TILE + 1
    tok_ids = jnp.arange(n_tok, dtype=I32)

    kp, vp, fp, ksm, vsm, fsm = [], [], [], [], [], []
    for l in range(depth):
        lam_init = 0.8 - 0.6 * math.exp(-0.3 * l)
        scal = jnp.full((1,), lam_init, F32)
        wf = jnp.pad(w_in[l, :, 3 * D_MODEL:], ((0, 0), (0, LANES - H_B))).astype(BF16)
        bfg = jnp.zeros((1, LANES), F32).at[0, :H_B].set(b_forget[l])
        lq1, lk1 = lambda_q1[l][None, :], lambda_k1[l][None, :]
        lq2, lk2 = lambda_q2[l][None, :], lambda_k2[l][None, :]
        gsub = g_subln[l][None, :]

        xb, lf, lft = _prep(x, wf, bfg)
        qb, kf, vf, kb, vb = _qkv(xb, w_in, l, rope_c, rope_sa, rope_sb,
                                  n_prompt_tiles, tiles_per_seq)
        cumt = _cumsum(lft, u_incl, batch, seq)
        o_p = _attn(scal, qb, kb, vb, cumt, u_strict, lq1, lk1, lq2, lk2, gsub, batch, seq)

        q_s = qb[n_prompt:n_real].reshape(dec_b, dec_t, D_MODEL)
        qbd = jnp.where(qmask[None], jnp.swapaxes(q_s, 1, 2)[:, :, qidx], jnp.zeros((), BF16))
        pad = ((0, 0), (0, page - dec_t), (0, 0))
        kn = jnp.pad(kb[n_prompt:n_real].reshape(dec_b, dec_t, D_MODEL), pad)
        vn = jnp.pad(vb[n_prompt:n_real].reshape(dec_b, dec_t, D_MODEL), pad)
        lf_s = jnp.swapaxes(lf[n_prompt:n_real].reshape(dec_b, dec_t, LANES), 1, 2)
        lfn = jnp.where(col_is_b[None, :, None], lf_s[:, col_hb, :], 0.0)
        lfn = jnp.pad(lfn, ((0, 0), (0, 0), (0, page - dec_t)))
        o_s = _sattn(page_table, scal, l, ck, cv, cft, qbd, kn, vn, lfn, ut_strict,
                     lq1, lk1, lq2, lk2, gsub)
        o_s = jnp.swapaxes(o_s[:, :, :dec_t, :], 1, 2).reshape(n_sample, D_MODEL).astype(BF16)
        o_s = jnp.pad(o_s, ((0, ROW_TILE - n_sample), (0, 0)))

        x1, logits = _oproj(o_p, o_s, wo_b, l, x, ln1_g[l][None, :],
                            ln1_b[l][None, :], wrh, wrl, br, alpha)
        route, cnt = _router(logits, low_strict, n_real)

        e12 = route[:, 0:2].astype(I32)
        rank = route[:, 4:6].astype(I32)
        counts = cnt[0, :N_EXPERTS].astype(I32)
        padded = ((counts + ROW_TILE - 1) // ROW_TILE) * ROW_TILE
        ends = jnp.cumsum(padded)
        starts = ends - padded
        valid = (tok_ids < n_real)[:, None]
        slot = jnp.where(valid, starts[e12] + rank, 0)
        tile_row0 = jnp.arange(n_tiles, dtype=I32) * ROW_TILE
        tile_expert = jnp.minimum(jnp.sum(tile_row0[:, None] >= ends[None, :], axis=1),
                                  N_EXPERTS - 1).astype(I32)
        n_used = (ends[-1] // ROW_TILE).astype(I32).reshape(1)
        scatter_to = jnp.where(valid, slot, n_tiles * ROW_TILE).reshape(-1)
        tok_of_slot = jnp.zeros((n_tiles * ROW_TILE,), I32).at[scatter_to].set(
            jnp.repeat(tok_ids, 2), mode='drop')

        ys = _experts(tile_expert, n_used, tok_of_slot, x1, wgu_b, wd_b, l, n_tiles)
        x = _combine(slot[:, 0], slot[:, 1], x1, route, ln2_g[l][None, :], ln2_b[l][None, :],
                     ys, alpha)

        kp.append(kf[:n_prompt].reshape(batch, seq, H_TOT, HEAD_DIM))
        vp.append(vf[:n_prompt].reshape(batch, seq, H_TOT, HEAD_DIM))
        fp.append(lf[:n_prompt, :H_B].reshape(batch, seq, H_B))
        ksm.append(kf[n_prompt:n_real].reshape(dec_b, dec_t, H_TOT, HEAD_DIM))
        vsm.append(vf[n_prompt:n_real].reshape(dec_b, dec_t, H_TOT, HEAD_DIM))
        fsm.append(lf[n_prompt:n_real, :H_B].reshape(dec_b, dec_t, H_B))

    y_p = x[:n_prompt].reshape(batch, seq, D_MODEL)
    y_s = x[n_prompt:n_real].reshape(dec_b, dec_t, D_MODEL)
    return (y_p, y_s, jnp.stack(kp), jnp.stack(vp), jnp.stack(fp),
            jnp.stack(ksm), jnp.stack(vsm), jnp.stack(fsm))
```

```python
import functools
import math

import jax
import jax.numpy as jnp
from jax import lax
from jax.experimental import pallas as pl
from jax.experimental.pallas import tpu as pltpu

F32 = jnp.float32
BF16 = jnp.bfloat16
I32 = jnp.int32

HEAD_DIM = 128
H_TOT = 16
H_A = 6
H_B = 5
H_C = 5
D_MODEL = H_TOT * HEAD_DIM
DQK_A = HEAD_DIM // 2
ROT_A = DQK_A // 4
ROPE_THETA = 500000.0
N_EXPERTS = 16
N_GROUPS = 4
EXPERTS_PER_GROUP = 4
D_FF = D_MODEL // 2
LN_EPS = 1e-5
NEG = -1e30

LANES = 128
ROW_TILE = 256
HEADS_PER_STEP = 4
ATT_BLOCK = 512
SB_KBLOCK = 256
PAGES_PER_STEP = 4
COLS_PER_HEAD = 8
GATHER_UNROLL = 8
VMEM_LIMIT = 56 * 1024 * 1024


def _dot(a, b):
    return jnp.dot(a, b, preferred_element_type=F32)


def _dot_nt(a, b):
    return lax.dot_general(a, b, (((1,), (1,)), ((), ())), preferred_element_type=F32)


def _log_sigmoid(x):
    return jnp.minimum(x, 0.0) - jnp.log1p(jnp.exp(-jnp.abs(x)))


def _log_sigmoid_scores(x):
    return jnp.minimum(x, 0.0) - jnp.log(1.0 + jnp.exp(-jnp.abs(x)))


def _split2(x):
    hi = x.astype(BF16)
    lo = (x - hi.astype(F32)).astype(BF16)
    return hi, lo


def _layer_norm(y, g, b):
    mu = jnp.mean(y, axis=-1, keepdims=True)
    d = y - mu
    var = jnp.mean(d * d, axis=-1, keepdims=True)
    return d * lax.rsqrt(var + LN_EPS) * g + b


def _prep_kernel(x_ref, wf_ref, bf_ref, xb_ref, lf_ref, lft_ref):
    xb = x_ref[...].astype(BF16)
    xb_ref[...] = xb
    lf = _log_sigmoid(_dot(xb, wf_ref[...]) + bf_ref[...])
    lf_ref[...] = lf
    lft_ref[...] = lf.T[:8, :]


def _prep(x, wf, bfg):
    n = x.shape[0]
    nt = n // ROW_TILE
    return pl.pallas_call(
        _prep_kernel,
        grid=(nt,),
        in_specs=[pl.BlockSpec((ROW_TILE, D_MODEL), lambda i: (i, 0)),
                  pl.BlockSpec((D_MODEL, LANES), lambda i: (0, 0)),
                  pl.BlockSpec((1, LANES), lambda i: (0, 0))],
        out_specs=[pl.BlockSpec((ROW_TILE, D_MODEL), lambda i: (i, 0)),
                   pl.BlockSpec((ROW_TILE, LANES), lambda i: (i, 0)),
                   pl.BlockSpec((8, ROW_TILE), lambda i: (0, i))],
        out_shape=[jax.ShapeDtypeStruct((n, D_MODEL), BF16),
                   jax.ShapeDtypeStruct((n, LANES), F32),
                   jax.ShapeDtypeStruct((8, n), F32)],
        compiler_params=pltpu.CompilerParams(dimension_semantics=("parallel",)),
        name="prep",
    )(x, wf, bfg)


def _qkv_kernel(xb_ref, wq_ref, wk_ref, wv_ref, c_ref, sa_ref, sb_ref,
                q_ref, k_ref, v_ref, kb_ref, vb_ref, wb_scr):
    j = pl.program_id(0)

    @pl.when(pl.program_id(1) == 0)
    def _():
        wb_scr[0] = wq_ref[0].astype(BF16)
        wb_scr[1] = wk_ref[0].astype(BF16)
        wb_scr[2] = wv_ref[0].astype(BF16)

    xb = xb_ref[...]
    q = _dot(xb, wb_scr[0])
    k = _dot(xb, wb_scr[1])
    v = _dot(xb, wb_scr[2])
    v_ref[...] = v
    vb_ref[...] = v.astype(BF16)
    scale_a = DQK_A ** -0.5
    scale_bc = HEAD_DIM ** -0.5

    def rope(t):
        return (t * c_ref[...] + pltpu.roll(t, LANES - ROT_A // 2, 1) * sa_ref[...]
                + pltpu.roll(t, ROT_A // 2, 1) * sb_ref[...])

    @pl.when(j * HEADS_PER_STEP < H_A)
    def _():
        for hh in range(HEADS_PER_STEP):
            sl = slice(hh * HEAD_DIM, (hh + 1) * HEAD_DIM)
            use = (j * HEADS_PER_STEP + hh) < H_A
            qh = q[:, sl]
            kh = k[:, sl]
            qh = jnp.where(use, rope(qh), qh)
            kh = jnp.where(use, rope(kh), kh)
            scale = jnp.where(use, scale_a, scale_bc)
            q_ref[:, sl] = (qh * scale).astype(BF16)
            k_ref[:, sl] = kh
            kb_ref[:, sl] = kh.astype(BF16)

    @pl.when(j * HEADS_PER_STEP >= H_A)
    def _():
        q_ref[...] = (q * scale_bc).astype(BF16)
        k_ref[...] = k
        kb_ref[...] = k.astype(BF16)


def _qkv(xb, w_in, layer, rope_c, rope_sa, rope_sb, n_prompt_tiles, tiles_per_seq):
    n = xb.shape[0]
    nt = n // ROW_TILE
    wcol = HEADS_PER_STEP * HEAD_DIM
    nj = D_MODEL // wcol

    def tbl(j, i):
        return (jnp.where(i < n_prompt_tiles, i % tiles_per_seq, tiles_per_seq), 0)

    def wspec(part):
        return pl.BlockSpec((1, D_MODEL, wcol), lambda j, i: (layer, 0, part * nj + j))

    ospec = pl.BlockSpec((ROW_TILE, wcol), lambda j, i: (i, j))
    tspec = pl.BlockSpec((ROW_TILE, LANES), tbl)
    return pl.pallas_call(
        _qkv_kernel,
        grid=(nj, nt),
        in_specs=[pl.BlockSpec((ROW_TILE, D_MODEL), lambda j, i: (i, 0)),
                  wspec(0), wspec(1), wspec(2), tspec, tspec, tspec],
        out_specs=[ospec, ospec, ospec, ospec, ospec],
        scratch_shapes=[pltpu.VMEM((3, D_MODEL, wcol), BF16)],
        out_shape=[jax.ShapeDtypeStruct((n, D_MODEL), BF16),
                   jax.ShapeDtypeStruct((n, D_MODEL), F32),
                   jax.ShapeDtypeStruct((n, D_MODEL), F32),
                   jax.ShapeDtypeStruct((n, D_MODEL), BF16),
                   jax.ShapeDtypeStruct((n, D_MODEL), BF16)],
        compiler_params=pltpu.CompilerParams(dimension_semantics=("parallel", "arbitrary"),
                                             vmem_limit_bytes=VMEM_LIMIT),
        name="qkv",
    )(xb, w_in, w_in, w_in, rope_c, rope_sa, rope_sb)


def _cumsum_kernel(lft_ref, u_ref, out_ref):
    t = lft_ref.shape[1]
    blk = u_ref.shape[0]
    u = u_ref[...]
    carry = jnp.zeros((8, 1), F32)
    for b in range(t // blk):
        x = lft_ref[:, b * blk:(b + 1) * blk]
        hi = x.astype(BF16)
        r = x - hi.astype(F32)
        mid = r.astype(BF16)
        lo = (r - mid.astype(F32)).astype(BF16)
        cs = _dot(hi, u) + _dot(mid, u) + _dot(lo, u) + carry
        out_ref[:, b * blk:(b + 1) * blk] = cs
        carry = cs[:, blk - 1:blk]


def _cumsum(lft, u_incl, batch, seq):
    return pl.pallas_call(
        _cumsum_kernel,
        grid=(batch,),
        in_specs=[pl.BlockSpec((8, seq), lambda b: (0, b)),
                  pl.BlockSpec(u_incl.shape, lambda b: (0, 0))],
        out_specs=pl.BlockSpec((8, seq), lambda b: (0, b)),
        out_shape=jax.ShapeDtypeStruct((8, batch * seq), F32),
        compiler_params=pltpu.CompilerParams(dimension_semantics=("parallel",)),
        name="cumsum",
    )(lft, u_incl)


def _loop_pairs(n, step, carry):
    def pair(p, c):
        return step(2 * p + 1, step(2 * p, c))
    carry = lax.fori_loop(0, n // 2, pair, carry)
    return lax.cond(n % 2 == 1, lambda c: step(n - 1, c), lambda c: c, carry)


def _softmax_update(s, m, l, acc, v):
    m_new = jnp.maximum(m, jnp.max(s, axis=1, keepdims=True))
    alpha = jnp.exp(m - m_new)
    p = jnp.exp(s - m_new)
    l = alpha * l + jnp.sum(p, axis=1, keepdims=True)
    acc = alpha * acc + _dot(p.astype(BF16), v)
    return m_new, l, acc


def _attn_kernel(scal_ref, q_ref, k_ref, v_ref, cum_ref, u_ref,
                 lq1_ref, lk1_ref, lq2_ref, lk2_ref, g_ref, o_ref, *, tq, tkc):
    h = pl.program_id(1)
    qi = pl.program_id(2)
    lam_init = scal_ref[0]

    def kv(kb, width):
        start = pl.multiple_of(kb * width, width)
        return k_ref[pl.ds(start, width), :], v_ref[pl.ds(start, width), :]

    def causal(width_k, kb, strict):
        row = qi * tq + lax.broadcasted_iota(I32, (tq, width_k), 0)
        col = kb * width_k + lax.broadcasted_iota(I32, (tq, width_k), 1)
        return (col < row) if strict else (col <= row)

    def init():
        return (jnp.full((tq, 1), NEG, F32), jnp.zeros((tq, 1), F32),
                jnp.zeros((tq, HEAD_DIM), F32))

    @pl.when(h < H_A)
    def _():
        q = q_ref[...]
        lane = lax.broadcasted_iota(I32, q.shape, 1)
        zero = jnp.zeros_like(q)
        q0 = jnp.where(lane < DQK_A, q, zero)
        q1 = jnp.where(lane >= DQK_A, q, zero)

        def step(kb, carry, masked):
            c0, c1 = carry
            k, v = kv(kb, tq)
            s0 = _dot_nt(q0, k)
            s1 = _dot_nt(q1, k)
            if masked:
                ok = causal(tq, kb, False)
                s0 = jnp.where(ok, s0, NEG)
                s1 = jnp.where(ok, s1, NEG)
            return _softmax_update(s0, *c0, v), _softmax_update(s1, *c1, v)

        carry = _loop_pairs(qi, lambda kb, c: step(kb, c, False), (init(), init()))
        (_, l0, a0), (_, l1, a1) = step(qi, carry, True)
        lam = (jnp.exp(jnp.sum(lq1_ref[...] * lk1_ref[...], axis=1, keepdims=True))
               - jnp.exp(jnp.sum(lq2_ref[...] * lk2_ref[...], axis=1, keepdims=True)) + lam_init)
        o = a0 / l0 - lam * (a1 / l1)
        o = o * lax.rsqrt(jnp.mean(o * o, axis=1, keepdims=True) + LN_EPS) * g_ref[...]
        o_ref[...] = (o * (1.0 - lam_init)).astype(o_ref.dtype)

    @pl.when((h >= H_A) & (h < H_A + H_B))
    def _():
        q = q_ref[...]
        hb = h - H_A

        def step(kb, carry, masked):
            k, v = kv(kb, tq)
            start = pl.multiple_of(kb * tq, tq)
            s = _dot_nt(q, k) - cum_ref[pl.ds(hb, 1), pl.ds(start, tq)]
            if masked:
                s = jnp.where(causal(tq, kb, False), s, NEG)
            return _softmax_update(s, *carry, v)

        carry = _loop_pairs(qi, lambda kb, c: step(kb, c, False), init())
        _, l, acc = step(qi, carry, True)
        o_ref[...] = (acc / l).astype(o_ref.dtype)

    @pl.when(h >= H_A + H_B)
    def _():
        q = q_ref[...]
        u = u_ref[...]
        ratio = tq // tkc

        def step(kb, carry, masked):
            run, acc = carry
            k, v = kv(kb, tkc)
            z = _dot_nt(q, k)
            lsz = _log_sigmoid_scores(z)
            ls = lsz - z
            if masked:
                ok = causal(tkc, kb, True)
                ls = jnp.where(ok, ls, 0.0)
            hi, lo = _split2(ls)
            between = _dot(hi, u) + _dot(lo, u) + run
            w = jnp.exp(lsz + between)
            if masked:
                w = jnp.where(ok, w, 0.0)
            acc = acc + _dot(w.astype(BF16), v)
            run = run + jnp.sum(ls, axis=1, keepdims=True)
            return run, acc

        carry = (jnp.zeros((tq, 1), F32), jnp.zeros((tq, HEAD_DIM), F32))
        for d in range(ratio):
            carry = step(qi * ratio + (ratio - 1 - d), carry, True)
        nfull = qi * ratio
        carry = _loop_pairs(nfull, lambda i, c: step(nfull - 1 - i, c, False), carry)
        o_ref[...] = carry[1].astype(o_ref.dtype)


def _attn(scal, qb, kb, vb, cumt, u_strict, lq1, lk1, lq2, lk2, g, batch, seq):
    tq = min(ATT_BLOCK, seq)
    tkc = min(SB_KBLOCK, tq)
    nq = seq // tq
    lspec = pl.BlockSpec((1, DQK_A), lambda b, h, i: (0, 0))
    return pl.pallas_call(
        functools.partial(_attn_kernel, tq=tq, tkc=tkc),
        grid=(batch, H_TOT, nq),
        in_specs=[pl.BlockSpec(memory_space=pltpu.SMEM),
                  pl.BlockSpec((tq, HEAD_DIM), lambda b, h, i: (b * nq + i, h)),
                  pl.BlockSpec((seq, HEAD_DIM), lambda b, h, i: (b, h)),
                  pl.BlockSpec((seq, HEAD_DIM), lambda b, h, i: (b, h)),
                  pl.BlockSpec((8, seq), lambda b, h, i: (0, b)),
                  pl.BlockSpec((tkc, tkc), lambda b, h, i: (0, 0)),
                  lspec, lspec, lspec, lspec,
                  pl.BlockSpec((1, HEAD_DIM), lambda b, h, i: (0, 0))],
        out_specs=pl.BlockSpec((tq, HEAD_DIM), lambda b, h, i: (b * nq + i, h)),
        out_shape=jax.ShapeDtypeStruct((batch * seq, D_MODEL), BF16),
        compiler_params=pltpu.CompilerParams(
            dimension_semantics=("parallel", "parallel", "parallel"),
            vmem_limit_bytes=VMEM_LIMIT),
        name="attn",
    )(scal, qb, kb, vb, cumt, u_strict, lq1, lk1, lq2, lk2, g)


def _sattn_kernel(*refs, n_pages_step, page, layer):
    g_n = n_pages_step
    pt_ref, scal_ref, ck_hbm, cv_hbm = refs[:4]
    f_refs = refs[4:4 + g_n]
    (qbd_ref, kn_ref, vn_ref, lfn_ref, ut_ref,
     lq1_ref, lk1_ref, lq2_ref, lk2_ref, gs_ref,
     o_ref, m_scr, l_scr, run_scr, acc_scr, vb_scr, st_scr, lft_scr,
     kbuf, vbuf, sem) = refs[4 + g_n:]
    b = pl.program_id(0)
    nb = pl.num_programs(0)
    t = pl.program_id(1)
    nsteps = pl.num_programs(1)
    slot = (b * nsteps + t) % 2

    def page_copies(b_, t_, slot_):
        out = []
        for g in range(g_n):
            pid = pt_ref[b_, (nsteps - 1 - t_) * g_n + g]
            rows = pl.ds(g * page, page)
            for h in range(H_TOT):
                lanes = pl.ds(h * HEAD_DIM, HEAD_DIM)
                out.append(pltpu.make_async_copy(ck_hbm.at[layer, pid, :, h, :],
                                                 kbuf.at[slot_, rows, lanes], sem.at[0, slot_]))
                out.append(pltpu.make_async_copy(cv_hbm.at[layer, pid, :, h, :],
                                                 vbuf.at[slot_, rows, lanes], sem.at[1, slot_]))
        return out

    @pl.when((b == 0) & (t == 0))
    def _():
        for c in page_copies(b, t, slot):
            c.start()

    @pl.when((b < nb - 1) | (t < nsteps - 1))
    def _():
        last_t = t == nsteps - 1
        for c in page_copies(jnp.where(last_t, b + 1, b), jnp.where(last_t, 0, t + 1), 1 - slot):
            c.start()
    lam_init = scal_ref[0]
    ncol = H_TOT * COLS_PER_HEAD
    col_a = H_A * COLS_PER_HEAD
    col_b = (H_A + H_B) * COLS_PER_HEAD

    def process(width, valid_ab, valid_c):
        st = st_scr[:, :width]
        lft = lft_scr[:, :width]
        row = lax.broadcasted_iota(I32, (ncol, width), 0)
        is_a = row < col_a
        is_c = row >= col_b
        lsz = _log_sigmoid_scores(st)
        ls = lsz - st
        if valid_c is not None:
            ls = jnp.where(valid_c, ls, 0.0)
        x = jnp.where(is_c, ls, lft)
        hi, lo = _split2(x)
        ut = ut_ref[:width, :width]
        suffix = _dot(hi, ut) + _dot(lo, ut) + run_scr[...]
        logit = jnp.where(is_a, st, jnp.where(is_c, lsz, st) + suffix)
        if valid_ab is not None:
            logit = jnp.where((is_c & valid_c) | (~is_c & valid_ab), logit, NEG)
        m_old = m_scr[...]
        m_new = jnp.where(is_c[:, :1], 0.0,
                          jnp.maximum(m_old, jnp.max(logit, axis=1, keepdims=True)))
        p = jnp.exp(logit - m_new)
        alpha = jnp.exp(m_old - m_new)
        l_scr[...] = alpha * l_scr[...] + jnp.sum(p, axis=1, keepdims=True)
        m_scr[...] = m_new
        run_scr[...] = run_scr[...] + jnp.sum(x, axis=1, keepdims=True)
        pb = p.astype(BF16)
        for hp in range(H_TOT // 2):
            rs = slice(2 * COLS_PER_HEAD * hp, 2 * COLS_PER_HEAD * (hp + 1))
            cs = slice(2 * HEAD_DIM * hp, 2 * HEAD_DIM * (hp + 1))
            acc_scr[rs, :] = alpha[rs, :] * acc_scr[rs, :] + _dot(pb[rs, :], vb_scr[:width, cs])

    @pl.when(t == 0)
    def _():
        row = lax.broadcasted_iota(I32, (ncol, 1), 0)
        m_scr[...] = jnp.where(row >= col_b, 0.0, NEG)
        l_scr[...] = jnp.zeros_like(l_scr)
        run_scr[...] = jnp.zeros_like(run_scr)
        acc_scr[...] = jnp.zeros_like(acc_scr)
        lft_scr[...] = jnp.zeros_like(lft_scr)
        st_scr[:, :page] = _dot(kn_ref[0], qbd_ref[0]).T
        lft_scr[:, :page] = lfn_ref[0]
        vb_scr[:page, :] = vn_ref[0]
        qidx = lax.broadcasted_iota(I32, (ncol, page), 0) % 4
        key = lax.broadcasted_iota(I32, (ncol, page), 1)
        process(page, key <= qidx, key < qidx)

    for c in page_copies(b, t, slot):
        c.wait()
    for g in range(g_n):
        sl = slice(g * page, (g + 1) * page)
        st_scr[:, sl] = _dot(kbuf[slot, sl, :].astype(BF16), qbd_ref[0]).T
        vb_scr[sl, :] = vbuf[slot, sl, :].astype(BF16)
        for hb in range(H_B):
            r0 = (H_A + hb) * COLS_PER_HEAD
            lft_scr[r0:r0 + COLS_PER_HEAD, sl] = jnp.broadcast_to(
                f_refs[g][0, 0, hb:hb + 1, :], (COLS_PER_HEAD, page))
    process(g_n * page, None, None)

    @pl.when(t == nsteps - 1)
    def _():
        row = lax.broadcasted_iota(I32, (ncol, 1), 0)
        linv = jnp.where(row >= col_b, 1.0, 1.0 / l_scr[...])
        lam = (jnp.exp(jnp.sum(lq1_ref[...] * lk1_ref[...], axis=1, keepdims=True))
               - jnp.exp(jnp.sum(lq2_ref[...] * lk2_ref[...], axis=1, keepdims=True)) + lam_init)
        half = COLS_PER_HEAD // 2
        for h in range(H_TOT):
            rs = slice(COLS_PER_HEAD * h, COLS_PER_HEAD * (h + 1))
            cs = slice(HEAD_DIM * (h % 2), HEAD_DIM * (h % 2 + 1))
            blk = acc_scr[rs, cs] * linv[rs, :]
            if h < H_A:
                o = blk - lam * pltpu.roll(blk, half, 0)
                o = o * lax.rsqrt(jnp.mean(o * o, axis=1, keepdims=True) + LN_EPS) * gs_ref[...]
                blk = o * (1.0 - lam_init)
            o_ref[0, h] = blk


def _sattn(page_table, scal, layer, ck, cv, cft, qbd, kn, vn, lfn, ut_strict,
           lq1, lk1, lq2, lk2, g):
    nb, n_pages = page_table.shape
    page = cft.shape[3]
    g_n = PAGES_PER_STEP
    nsteps = n_pages // g_n
    ncol = H_TOT * COLS_PER_HEAD

    def page_map(gg):
        def f(b, t, pt):
            return (layer, pt[b, (nsteps - 1 - t) * g_n + gg], 0, 0)
        return f

    fspecs = [pl.BlockSpec((1, 1, 8, page), page_map(gg)) for gg in range(g_n)]
    per_b3 = lambda b, t, pt: (b, 0, 0)
    const2 = lambda b, t, pt: (0, 0)
    lspec = pl.BlockSpec((1, DQK_A), const2)
    in_specs = ([pl.BlockSpec(memory_space=pltpu.SMEM), pl.BlockSpec(memory_space=pl.ANY),
                 pl.BlockSpec(memory_space=pl.ANY)] + fspecs + [
        pl.BlockSpec((1, D_MODEL, ncol), per_b3),
        pl.BlockSpec((1, page, D_MODEL), per_b3),
        pl.BlockSpec((1, page, D_MODEL), per_b3),
        pl.BlockSpec((1, ncol, page), per_b3),
        pl.BlockSpec(ut_strict.shape, const2),
        lspec, lspec, lspec, lspec,
        pl.BlockSpec((1, HEAD_DIM), const2)])
    width = g_n * page
    grid_spec = pltpu.PrefetchScalarGridSpec(
        num_scalar_prefetch=1,
        grid=(nb, nsteps),
        in_specs=in_specs,
        out_specs=pl.BlockSpec((1, H_TOT, COLS_PER_HEAD, HEAD_DIM), lambda b, t, pt: (b, 0, 0, 0)),
        scratch_shapes=[pltpu.VMEM((ncol, 1), F32), pltpu.VMEM((ncol, 1), F32),
                        pltpu.VMEM((ncol, 1), F32), pltpu.VMEM((ncol, 2 * HEAD_DIM), F32),
                        pltpu.VMEM((width, D_MODEL), BF16),
                        pltpu.VMEM((ncol, width), F32), pltpu.VMEM((ncol, width), F32),
                        pltpu.VMEM((2, width, D_MODEL), F32), pltpu.VMEM((2, width, D_MODEL), F32),
                        pltpu.SemaphoreType.DMA((2, 2))])
    return pl.pallas_call(
        functools.partial(_sattn_kernel, n_pages_step=g_n, page=page, layer=layer),
        grid_spec=grid_spec,
        out_shape=jax.ShapeDtypeStruct((nb, H_TOT, COLS_PER_HEAD, HEAD_DIM), F32),
        compiler_params=pltpu.CompilerParams(dimension_semantics=("arbitrary", "arbitrary"),
                                             vmem_limit_bytes=VMEM_LIMIT),
        name="sattn",
    )(page_table, scal, ck, cv, *([cft] * g_n),
      qbd, kn, vn, lfn, ut_strict, lq1, lk1, lq2, lk2, g)


def _oproj_kernel(op_ref, os_ref, wo_ref, x_ref, g_ref, b_ref, wrh_ref, wrl_ref, br_ref,
                  x1_ref, lg_ref, *, alpha, n_prompt_tiles):
    o = jnp.where(pl.program_id(0) < n_prompt_tiles, op_ref[...], os_ref[...])
    mix = _dot(o, wo_ref[0])
    x1 = _layer_norm(alpha * x_ref[...] + mix, g_ref[...], b_ref[...])
    x1_ref[...] = x1
    xh, xl = _split2(x1)
    wrh = wrh_ref[...]
    lg_ref[...] = _dot(xh, wrh) + _dot(xl, wrh) + _dot(xh, wrl_ref[...]) + br_ref[...]


def _oproj(o_p, o_s, wo, layer, x, g, b, wrh, wrl, br, alpha):
    n = x.shape[0]
    nt = n // ROW_TILE
    n_prompt_tiles = o_p.shape[0] // ROW_TILE
    row = lambda i: (i, 0)
    const = lambda i: (0, 0)
    return pl.pallas_call(
        functools.partial(_oproj_kernel, alpha=alpha, n_prompt_tiles=n_prompt_tiles),
        grid=(nt,),
        in_specs=[pl.BlockSpec((ROW_TILE, D_MODEL), lambda i: (jnp.minimum(i, n_prompt_tiles - 1), 0)),
                  pl.BlockSpec((ROW_TILE, D_MODEL), const),
                  pl.BlockSpec((1, D_MODEL, D_MODEL), lambda i: (layer, 0, 0)),
                  pl.BlockSpec((ROW_TILE, D_MODEL), row),
                  pl.BlockSpec((1, D_MODEL), const),
                  pl.BlockSpec((1, D_MODEL), const),
                  pl.BlockSpec((D_MODEL, LANES), const),
                  pl.BlockSpec((D_MODEL, LANES), const),
                  pl.BlockSpec((1, LANES), const)],
        out_specs=[pl.BlockSpec((ROW_TILE, D_MODEL), row),
                   pl.BlockSpec((ROW_TILE, LANES), row)],
        out_shape=[jax.ShapeDtypeStruct((n, D_MODEL), F32),
                   jax.ShapeDtypeStruct((n, LANES), F32)],
        compiler_params=pltpu.CompilerParams(dimension_semantics=("parallel",),
                                             vmem_limit_bytes=VMEM_LIMIT),
        name="oproj",
    )(o_p, o_s, wo, x, g, b, wrh, wrl, br)


def _router_kernel(lg_ref, low_ref, route_ref, cnt_ref, run_scr, *, n_real):
    i = pl.program_id(0)
    tm = lg_ref.shape[0]

    @pl.when(i == 0)
    def _():
        run_scr[...] = jnp.zeros_like(run_scr)

    lg = lg_ref[...]
    lane = lax.broadcasted_iota(I32, lg.shape, 1)
    lane_f = lane.astype(F32)
    tok = i * tm + lax.broadcasted_iota(I32, lg.shape, 0)
    is_e = lane < N_EXPERTS
    lgm = jnp.where(is_e, lg, NEG)
    ex = jnp.where(is_e, jnp.exp(lgm - jnp.max(lgm, axis=1, keepdims=True)), 0.0)
    probs = ex / jnp.sum(ex, axis=1, keepdims=True)

    best = None
    for grp in range(N_GROUPS):
        ing = (lane >= grp * EXPERTS_PER_GROUP) & (lane < (grp + 1) * EXPERTS_PER_GROUP)
        pg = jnp.where(ing, probs, -1.0)
        p1 = jnp.max(pg, axis=1, keepdims=True)
        i1 = jnp.min(jnp.where(pg == p1, lane_f, 1e9), axis=1, keepdims=True)
        pg2 = jnp.where(lane_f == i1, -1.0, pg)
        p2 = jnp.max(pg2, axis=1, keepdims=True)
        i2 = jnp.min(jnp.where(pg2 == p2, lane_f, 1e9), axis=1, keepdims=True)
        cand = (p1 + p2, p1, i1, p2, i2)
        if best is None:
            best = cand
        else:
            better = cand[0] > best[0]
            best = tuple(jnp.where(better, c, o) for c, o in zip(cand, best))
    _, p1, i1, p2, i2 = best
    den = p1 + p2
    g1 = p1 / den
    g2 = p2 / den

    sel1 = lane_f == i1
    sel2 = lane_f == i2
    oh = jnp.where((sel1 | sel2) & (tok < n_real), 1.0, 0.0)
    total = _dot(low_ref[...], oh.astype(BF16)) + run_scr[...]
    r1 = jnp.sum(jnp.where(sel1, total, 0.0), axis=1, keepdims=True)
    r2 = jnp.sum(jnp.where(sel2, total, 0.0), axis=1, keepdims=True)
    run = run_scr[...] + jnp.sum(oh, axis=0, keepdims=True)
    run_scr[...] = run
    cnt_ref[...] = jnp.broadcast_to(run, cnt_ref.shape)
    route = jnp.zeros_like(lg)
    for pos, val in enumerate((i1, i2, g1, g2, r1, r2)):
        route = jnp.where(lane == pos, val, route)
    route_ref[...] = route


def _router(logits, low_strict, n_real):
    n = logits.shape[0]
    nt = n // ROW_TILE
    return pl.pallas_call(
        functools.partial(_router_kernel, n_real=n_real),
        grid=(nt,),
        in_specs=[pl.BlockSpec((ROW_TILE, LANES), lambda i: (i, 0)),
                  pl.BlockSpec((ROW_TILE, ROW_TILE), lambda i: (0, 0))],
        out_specs=[pl.BlockSpec((ROW_TILE, LANES), lambda i: (i, 0)),
                   pl.BlockSpec((8, LANES), lambda i: (0, 0))],
        out_shape=[jax.ShapeDtypeStruct((n, LANES), F32),
                   jax.ShapeDtypeStruct((8, LANES), F32)],
        scratch_shapes=[pltpu.VMEM((1, LANES), F32)],
        compiler_params=pltpu.CompilerParams(dimension_semantics=("arbitrary",)),
        name="router",
    )(logits, low_strict)


def _row_gather_start(src_hbm, dst, sem, idx_ref, base, n_rows):
    def body(r, c):
        pltpu.make_async_copy(src_hbm.at[pl.ds(idx_ref[base + r], 1), :],
                              dst.at[pl.ds(r, 1), :], sem).start()
        return c
    lax.fori_loop(0, n_rows, body, 0, unroll=GATHER_UNROLL)


def _row_gather_wait(src_hbm, dst, sem, n_rows):
    def body(r, c):
        pltpu.make_async_copy(src_hbm.at[pl.ds(0, 1), :], dst.at[pl.ds(r, 1), :], sem).wait()
        return c
    lax.fori_loop(0, n_rows, body, 0, unroll=GATHER_UNROLL)


def _experts_kernel(te_ref, nu_ref, tok_ref, x_hbm, wgu_ref, wd_ref, ys_ref, xbuf, sem):
    del te_ref
    j = pl.program_id(0)
    nt = pl.num_programs(0)
    tm = ys_ref.shape[0]
    n_used = nu_ref[0]
    slot = j % 2

    @pl.when(j == 0)
    def _():
        _row_gather_start(x_hbm, xbuf.at[0], sem.at[0], tok_ref, 0, tm)

    @pl.when((j + 1 < nt) & (j + 1 < n_used))
    def _():
        _row_gather_start(x_hbm, xbuf.at[1 - slot], sem.at[1 - slot], tok_ref, (j + 1) * tm, tm)

    @pl.when(j < n_used)
    def _():
        _row_gather_wait(x_hbm, xbuf.at[slot], sem.at[slot], tm)
        xb = xbuf[slot].astype(BF16)
        hgu = _dot(xb, wgu_ref[0, 0])
        gate = hgu[:, :D_FF]
        act = gate * jax.nn.sigmoid(gate) * hgu[:, D_FF:]
        ys_ref[...] = _dot(act.astype(BF16), wd_ref[0, 0])

    @pl.when(j >= n_used)
    def _():
        ys_ref[...] = jnp.zeros_like(ys_ref)


def _experts(tile_expert, n_used, tok_of_slot, x1, wgu, wd, layer, n_tiles):
    grid_spec = pltpu.PrefetchScalarGridSpec(
        num_scalar_prefetch=3,
        grid=(n_tiles,),
        in_specs=[pl.BlockSpec(memory_space=pl.ANY),
                  pl.BlockSpec((1, 1, D_MODEL, 2 * D_FF),
                               lambda j, te, nu, tk: (layer, te[j], 0, 0)),
                  pl.BlockSpec((1, 1, D_FF, D_MODEL),
                               lambda j, te, nu, tk: (layer, te[j], 0, 0))],
        out_specs=pl.BlockSpec((ROW_TILE, D_MODEL), lambda j, te, nu, tk: (j, 0)),
        scratch_shapes=[pltpu.VMEM((2, ROW_TILE, D_MODEL), F32),
                        pltpu.SemaphoreType.DMA((2,))])
    return pl.pallas_call(
        _experts_kernel,
        grid_spec=grid_spec,
        out_shape=jax.ShapeDtypeStruct((n_tiles * ROW_TILE, D_MODEL), F32),
        compiler_params=pltpu.CompilerParams(dimension_semantics=("arbitrary",),
                                             vmem_limit_bytes=VMEM_LIMIT),
        name="experts",
    )(tile_expert, n_used, tok_of_slot, x1, wgu, wd)


def _combine_kernel(s1_ref, s2_ref, x1_ref, route_ref, g_ref, b_ref, ys_hbm, out_ref,
                    ybuf, sem, *, alpha):
    i = pl.program_id(0)
    nt = pl.num_programs(0)
    tm = out_ref.shape[0]
    slot = i % 2

    def start(tile, s):
        _row_gather_start(ys_hbm, ybuf.at[s, 0], sem.at[s, 0], s1_ref, tile * tm, tm)
        _row_gather_start(ys_hbm, ybuf.at[s, 1], sem.at[s, 1], s2_ref, tile * tm, tm)

    @pl.when(i == 0)
    def _():
        start(0, 0)

    @pl.when(i + 1 < nt)
    def _():
        start(i + 1, 1 - slot)

    route = route_ref[...]
    g1 = route[:, 2:3]
    g2 = route[:, 3:4]
    _row_gather_wait(ys_hbm, ybuf.at[slot, 0], sem.at[slot, 0], tm)
    _row_gather_wait(ys_hbm, ybuf.at[slot, 1], sem.at[slot, 1], tm)
    y = g1 * ybuf[slot, 0] + g2 * ybuf[slot, 1]
    out_ref[...] = _layer_norm(alpha * x1_ref[...] + y, g_ref[...], b_ref[...])


def _combine(slot1, slot2, x1, route, g, b, ys, alpha):
    n = x1.shape[0]
    nt = n // ROW_TILE
    row = lambda i, s1, s2: (i, 0)
    const = lambda i, s1, s2: (0, 0)
    grid_spec = pltpu.PrefetchScalarGridSpec(
        num_scalar_prefetch=2,
        grid=(nt,),
        in_specs=[pl.BlockSpec((ROW_TILE, D_MODEL), row),
                  pl.BlockSpec((ROW_TILE, LANES), row),
                  pl.BlockSpec((1, D_MODEL), const),
                  pl.BlockSpec((1, D_MODEL), const),
                  pl.BlockSpec(memory_space=pl.ANY)],
        out_specs=pl.BlockSpec((ROW_TILE, D_MODEL), row),
        scratch_shapes=[pltpu.VMEM((2, 2, ROW_TILE, D_MODEL), F32),
                        pltpu.SemaphoreType.DMA((2, 2))])
    return pl.pallas_call(
        functools.partial(_combine_kernel, alpha=alpha),
        grid_spec=grid_spec,
        out_shape=jax.ShapeDtypeStruct((n, D_MODEL), F32),
        compiler_params=pltpu.CompilerParams(dimension_semantics=("arbitrary",),
                                             vmem_limit_bytes=VMEM_LIMIT),
        name="combine",
    )(slot1, slot2, x1, route, g, b, ys)


def _rope_tables(pos):
    half = ROT_A // 2
    inv_freq = ROPE_THETA ** (-jnp.arange(half, dtype=F32) * 2.0 / ROT_A)
    ang = pos.astype(F32)[:, None] * inv_freq[None, :]
    cos, sin = jnp.cos(ang), jnp.sin(ang)
    lane = jnp.arange(LANES)
    m = lane % half
    in_x1 = (lane % DQK_A) < half
    in_x2 = ((lane % DQK_A) >= half) & ((lane % DQK_A) < ROT_A)
    c = jnp.where((in_x1 | in_x2)[None, :], cos[:, m], 1.0)
    sa = jnp.where(in_x1[None, :], -sin[:, m], 0.0)
    sb = jnp.where(in_x2[None, :], sin[:, m], 0.0)
    return c, sa, sb


def _tri(n, fn):
    r = jnp.arange(n)[:, None]
    c = jnp.arange(n)[None, :]
    return fn(r, c).astype(BF16)


def _sample_col_layout():
    ncol = H_TOT * COLS_PER_HEAD
    col = jnp.arange(ncol)
    head = col // COLS_PER_HEAD
    r = col % COLS_PER_HEAD
    is_a = head < H_A
    used = is_a | (r < COLS_PER_HEAD // 2)
    lo = head * HEAD_DIM + jnp.where(is_a, (r // 4) * DQK_A, 0)
    hi = lo + jnp.where(is_a, DQK_A, HEAD_DIM)
    feat = jnp.arange(D_MODEL)[:, None]
    mask = (feat >= lo[None, :]) & (feat < hi[None, :]) & used[None, :]
    return mask, r % 4


def kernel(x_prompt, x_sample, cache_k, cache_v, cache_logf, page_table, w_in, b_forget,
           lambda_q1, lambda_k1, lambda_q2, lambda_k2, g_subln, w_out, ln1_g, ln1_b,
           w_router, b_router, w_gate_up, w_down, ln2_g, ln2_b):
    batch, seq, _ = x_prompt.shape
    dec_b, dec_t, _ = x_sample.shape
    depth = w_in.shape[0]
    n_pool, page = cache_k.shape[1], cache_k.shape[2]
    n_pages = page_table.shape[1]
    past_len = n_pages * page
    alpha = (2 * depth) ** 0.25

    n_prompt = batch * seq
    n_sample = dec_b * dec_t
    n_real = n_prompt + n_sample
    n_tok = n_prompt + ROW_TILE
    n_prompt_tiles = n_prompt // ROW_TILE
    tiles_per_seq = seq // ROW_TILE
    assert seq % ATT_BLOCK == 0 or seq < ATT_BLOCK
    assert seq % ROW_TILE == 0 and n_sample <= ROW_TILE and dec_t == 4
    assert n_pages % PAGES_PER_STEP == 0 and page == LANES

    x = jnp.concatenate([x_prompt.reshape(n_prompt, D_MODEL), x_sample.reshape(n_sample, D_MODEL),
                         jnp.zeros((ROW_TILE - n_sample, D_MODEL), F32)], axis=0)

    pos_s = past_len + (jnp.arange(ROW_TILE) % dec_t)
    rope_c, rope_sa, rope_sb = _rope_tables(jnp.concatenate([jnp.arange(seq), pos_s]))

    u_incl = _tri(SB_KBLOCK, lambda r, c: r <= c)
    u_strict = _tri(SB_KBLOCK, lambda r, c: r > c)
    ut_strict = _tri(PAGES_PER_STEP * page, lambda r, c: r > c)
    low_strict = _tri(ROW_TILE, lambda r, c: c < r)

    qmask, qidx = _sample_col_layout()
    ncol = H_TOT * COLS_PER_HEAD
    col_head = jnp.arange(ncol) // COLS_PER_HEAD
    col_is_b = (col_head >= H_A) & (col_head < H_A + H_B)
    col_hb = jnp.clip(col_head - H_A, 0, H_B - 1)

    cft = jnp.pad(jnp.swapaxes(cache_logf, 2, 3), ((0, 0), (0, 0), (0, 8 - H_B), (0, 0)))

    wr_pad = jnp.zeros((D_MODEL, LANES), F32).at[:, :N_EXPERTS].set(w_router)
    wrh = wr_pad.astype(BF16)
    wrl = (wr_pad - wrh.astype(F32)).astype(BF16)
    br = jnp.zeros((1, LANES), F32).at[0, :N_EXPERTS].set(b_router)

    wo_b = w_out.astype(BF16)
    wgu_b = w_gate_up.astype(BF16)
    wd_b = w_down.astype(BF16)

    n_tiles = (2 * n_real + N_EXPERTS * (ROW_TILE - 1)) // ROW_TILE + 1
    tok_ids = jnp.arange(n_tok, dtype=I32)

    kp, vp, fp, ksm, vsm, fsm = [], [], [], [], [], []
    for l in range(depth):
        lam_init = 0.8 - 0.6 * math.exp(-0.3 * l)
        scal = jnp.full((1,), lam_init, F32)
        wf = jnp.pad(w_in[l, :, 3 * D_MODEL:], ((0, 0), (0, LANES - H_B))).astype(BF16)
        bfg = jnp.zeros((1, LANES), F32).at[0, :H_B].set(b_forget[l])
        lq1, lk1 = lambda_q1[l][None, :], lambda_k1[l][None, :]
        lq2, lk2 = lambda_q2[l][None, :], lambda_k2[l][None, :]
        gsub = g_subln[l][None, :]

        xb, lf, lft = _prep(x, wf, bfg)
        qb, kf, vf, kb, vb = _qkv(xb, w_in, l, rope_c, rope_sa, rope_sb,
                                  n_prompt_tiles, tiles_per_seq)
        cumt = _cumsum(lft, u_incl, batch, seq)
        o_p = _attn(scal, qb, kb, vb, cumt, u_strict, lq1, lk1, lq2, lk2, gsub, batch, seq)

        q_s = qb[n_prompt:n_real].reshape(dec_b, dec_t, D_MODEL)
        qbd = jnp.where(qmask[None], jnp.swapaxes(q_s, 1, 2)[:, :, qidx], jnp.zeros((), BF16))
        pad = ((0, 0), (0, page - dec_t), (0, 0))
        kn = jnp.pad(kb[n_prompt:n_real].reshape(dec_b, dec_t, D_MODEL), pad)
        vn = jnp.pad(vb[n_prompt:n_real].reshape(dec_b, dec_t, D_MODEL), pad)
        lf_s = jnp.swapaxes(lf[n_prompt:n_real].reshape(dec_b, dec_t, LANES), 1, 2)
        lfn = jnp.where(col_is_b[None, :, None], lf_s[:, col_hb, :], 0.0)
        lfn = jnp.pad(lfn, ((0, 0), (0, 0), (0, page - dec_t)))
        o_s = _sattn(page_table, scal, l, cache_k, cache_v, cft, qbd, kn, vn, lfn, ut_strict,
                     lq1, lk1, lq2, lk2, gsub)
        o_s = jnp.swapaxes(o_s[:, :, :dec_t, :], 1, 2).reshape(n_sample, D_MODEL).astype(BF16)
        o_s = jnp.pad(o_s, ((0, ROW_TILE - n_sample), (0, 0)))

        x1, logits = _oproj(o_p, o_s, wo_b, l, x, ln1_g[l][None, :],
                            ln1_b[l][None, :], wrh, wrl, br, alpha)
        route, cnt = _router(logits, low_strict, n_real)

        e12 = route[:, 0:2].astype(I32)
        rank = route[:, 4:6].astype(I32)
        counts = cnt[0, :N_EXPERTS].astype(I32)
        padded = ((counts + ROW_TILE - 1) // ROW_TILE) * ROW_TILE
        ends = jnp.cumsum(padded)
        starts = ends - padded
        valid = (tok_ids < n_real)[:, None]
        slot = jnp.where(valid, starts[e12] + rank, 0)
        tile_row0 = jnp.arange(n_tiles, dtype=I32) * ROW_TILE
        tile_expert = jnp.minimum(jnp.sum(tile_row0[:, None] >= ends[None, :], axis=1),
                                  N_EXPERTS - 1).astype(I32)
        n_used = (ends[-1] // ROW_TILE).astype(I32).reshape(1)
        scatter_to = jnp.where(valid, slot, n_tiles * ROW_TILE).reshape(-1)
        tok_of_slot = jnp.zeros((n_tiles * ROW_TILE,), I32).at[scatter_to].set(
            jnp.repeat(tok_ids, 2), mode='drop')

        ys = _experts(tile_expert, n_used, tok_of_slot, x1, wgu_b, wd_b, l, n_tiles)
        x = _combine(slot[:, 0], slot[:, 1], x1, route, ln2_g[l][None, :], ln2_b[l][None, :],
                     ys, alpha)

        kp.append(kf[:n_prompt].reshape(batch, seq, H_TOT, HEAD_DIM))
        vp.append(vf[:n_prompt].reshape(batch, seq, H_TOT, HEAD_DIM))
        fp.append(lf[:n_prompt, :H_B].reshape(batch, seq, H_B))
        ksm.append(kf[n_prompt:n_real].reshape(dec_b, dec_t, H_TOT, HEAD_DIM))
        vsm.append(vf[n_prompt:n_real].reshape(dec_b, dec_t, H_TOT, HEAD_DIM))
        fsm.append(lf[n_prompt:n_real, :H_B].reshape(dec_b, dec_t, H_B))

    y_p = x[:n_prompt].reshape(batch, seq, D_MODEL)
    y_s = x[n_prompt:n_real].reshape(dec_b, dec_t, D_MODEL)
    return (y_p, y_s, jnp.stack(kp), jnp.stack(vp), jnp.stack(fp),
            jnp.stack(ksm), jnp.stack(vsm), jnp.stack(fsm))
```
